```python
import math
import jax, jax.numpy as jnp
from jax import lax
import numpy as np

D_MODEL = 1024
BATCH = 32
SEQ = 2048
DEPTH = 1

MIX_WIDTH = D_MODEL
ATTN_WIDTH = MIX_WIDTH // 2
CONV_WIDTH = MIX_WIDTH - ATTN_WIDTH
HEAD_DIM = 64
N_HEADS = ATTN_WIDTH // HEAD_DIM
DILATED_PATTERNS = ((128, 1), (512, 4), (2048, 16))
CONV_KERNEL = 31
N_EXPERTS = 16
CAPACITY_FACTOR = 2
D_FF_EXPERT = 2816
ROPE_THETA = 10000.0
EPS = 1e-6
NEG_INF = -1e30
IN_COLS = 3 * ATTN_WIDTH + 2 * CONV_WIDTH

kernel_name = "hybrid_dilated_attn_conformer_ec_moe"


def rms_norm(x, g):
    xf = x.astype(jnp.float32)
    y = xf * lax.rsqrt(jnp.mean(xf * xf, axis=-1, keepdims=True) + EPS)
    return (y * g.astype(jnp.float32)).astype(x.dtype)


def layer_norm(x, g, b):
    xf = x.astype(jnp.float32)
    mu = jnp.mean(xf, axis=-1, keepdims=True)
    xc = xf - mu
    y = xc * lax.rsqrt(jnp.mean(xc * xc, axis=-1, keepdims=True) + EPS)
    return (y * g.astype(jnp.float32) + b.astype(jnp.float32)).astype(x.dtype)


def rope(x, positions):
    half = x.shape[-1] // 2
    inv_freq = ROPE_THETA ** (-jnp.arange(half, dtype=jnp.float32) / half)
    ang = positions.astype(jnp.float32)[:, None] * inv_freq[None, :]
    cos, sin = jnp.cos(ang), jnp.sin(ang)
    xf = x.astype(jnp.float32)
    x1, x2 = xf[..., :half], xf[..., half:]
    out = jnp.concatenate([x1 * cos - x2 * sin, x2 * cos + x1 * sin], axis=-1)
    return out.astype(x.dtype)


def banded_attention(q, k, v, n):
    L, hd = q.shape[-2], q.shape[-1]
    lead = q.shape[:-2]
    nb = -(-L // n)
    Lp = nb * n
    lp = [(0, 0)] * len(lead)
    qb = jnp.pad(q, lp + [(0, Lp - L), (0, 0)]).reshape(*lead, nb, n, hd)

    def windows(t):
        tp = jnp.pad(t, lp + [(n, Lp - L + n), (0, 0)]).reshape(*lead, nb + 2, n, hd)
        return jnp.concatenate([tp[..., 0:nb, :, :], tp[..., 1:nb + 1, :, :],
                                tp[..., 2:nb + 2, :, :]], axis=-2)

    kw, vw = windows(k), windows(v)
    j = jnp.arange(nb)[:, None, None]
    qpos = j * n + jnp.arange(n)[None, :, None]
    kpos = j * n - n + jnp.arange(3 * n)[None, None, :]
    mask = (jnp.abs(qpos - kpos) <= n) & (kpos >= 0) & (kpos < L)
    s = jnp.einsum('...jqd,...jkd->...jqk', qb.astype(jnp.float32),
                   kw.astype(jnp.float32)) * (hd ** -0.5)
    s = jnp.where(mask, s, NEG_INF)
    m = jnp.max(s, axis=-1, keepdims=True)
    p = jnp.exp(s - m)
    l = jnp.sum(p, axis=-1, keepdims=True)
    o = jnp.einsum('...jqk,...jkd->...jqd', p, vw.astype(jnp.float32)) / l
    lse = (m + jnp.log(l))[..., 0]
    o = o.reshape(*lead, Lp, hd)[..., :L, :]
    lse = lse.reshape(*lead, Lp)[..., :L]
    return o, lse


def dilated_attention(q, k, v, window, dilation):
    B, H, S, hd = q.shape
    L = S // dilation

    def split(t):
        return t.reshape(B, H, L, dilation, hd).swapaxes(2, 3)

    o, lse = banded_attention(split(q), split(k), split(v), window // (2 * dilation))
    o = o.swapaxes(2, 3).reshape(B, H, S, hd)
    lse = lse.swapaxes(2, 3).reshape(B, H, S)
    return o, lse


def conformer_conv(a, b, dw_w, dw_b, ln_g, ln_b):
    u = a * jax.nn.sigmoid(b)
    C = u.shape[-1]
    pad = CONV_KERNEL // 2
    u = lax.conv_general_dilated(u, dw_w[:, None, :], window_strides=(1,),
                                 padding=[(pad, pad)],
                                 dimension_numbers=('NWC', 'WIO', 'NWC'),
                                 feature_group_count=C) + dw_b
    u = layer_norm(u, ln_g, ln_b)
    return jax.nn.silu(u)


def expert_choice_moe(h, w_router, w1, w3, w2):
    B, S, D = h.shape
    aff = jax.nn.softmax(jnp.einsum('bsd,de->bse', h, w_router).astype(jnp.float32), axis=-1)
    cap = CAPACITY_FACTOR * S // N_EXPERTS
    gate, idx = lax.top_k(aff.swapaxes(1, 2), cap)
    xe = jax.vmap(lambda hb, ib: hb[ib])(h, idx)

    def expert(args):
        xt, w1e, w3e, w2e = args
        return (jax.nn.silu(xt @ w1e) * (xt @ w3e)) @ w2e

    ye = lax.map(expert, (xe.swapaxes(0, 1), w1, w3, w2))
    ye = ye.swapaxes(0, 1) * gate[..., None].astype(ye.dtype)
    ye = ye.reshape(B, N_EXPERTS * cap, D)
    return jax.vmap(lambda y, i: jnp.zeros((S, D), y.dtype).at[i].add(y))(
        ye, idx.reshape(B, N_EXPERTS * cap))


def setup_inputs(seed: int = 0) -> dict:
    key = jax.random.key(seed)
    ks = jax.random.split(key, 16)
    f32 = jnp.float32
    nrm = lambda k, shape, scale: jax.random.normal(k, shape, f32) * scale
    return {
        "x": jax.random.normal(ks[0], (BATCH, SEQ, D_MODEL), f32),
        "norm1_g": 1.0 + nrm(ks[1], (D_MODEL,), 0.02),
        "w_in": nrm(ks[2], (D_MODEL, IN_COLS), D_MODEL ** -0.5),
        "q_norm_g": 1.0 + nrm(ks[3], (HEAD_DIM,), 0.02),
        "k_norm_g": 1.0 + nrm(ks[4], (HEAD_DIM,), 0.02),
        "conv_dw_w": nrm(ks[5], (CONV_KERNEL, CONV_WIDTH), CONV_KERNEL ** -0.5),
        "conv_dw_b": nrm(ks[6], (CONV_WIDTH,), 0.01),
        "conv_ln_g": 1.0 + nrm(ks[7], (CONV_WIDTH,), 0.02),
        "conv_ln_b": nrm(ks[8], (CONV_WIDTH,), 0.01),
        "w_out": nrm(ks[9], (MIX_WIDTH, D_MODEL), MIX_WIDTH ** -0.5),
        "norm2_g": 1.0 + nrm(ks[10], (D_MODEL,), 0.02),
        "w_router": nrm(ks[11], (D_MODEL, N_EXPERTS), D_MODEL ** -0.5),
        "w1": nrm(ks[12], (N_EXPERTS, D_MODEL, D_FF_EXPERT), D_MODEL ** -0.5),
        "w3": nrm(ks[13], (N_EXPERTS, D_MODEL, D_FF_EXPERT), D_MODEL ** -0.5),
        "w2": nrm(ks[14], (N_EXPERTS, D_FF_EXPERT, D_MODEL), D_FF_EXPERT ** -0.5),
    }


def reference(x, norm1_g, w_in, q_norm_g, k_norm_g, conv_dw_w, conv_dw_b,
              conv_ln_g, conv_ln_b, w_out, norm2_g, w_router, w1, w3, w2):
    B, S, D = x.shape
    positions = jnp.arange(S)
    for _ in range(DEPTH):
        h = rms_norm(x, norm1_g)
        z = jnp.einsum('bsd,dc->bsc', h, w_in)
        A = ATTN_WIDTH

        def heads(t):
            return t.reshape(B, S, N_HEADS, HEAD_DIM).transpose(0, 2, 1, 3)

        q = rope(rms_norm(heads(z[..., 0:A]), q_norm_g), positions)
        k = rope(rms_norm(heads(z[..., A:2 * A]), k_norm_g), positions)
        v = heads(z[..., 2 * A:3 * A])
        outs, lses = [], []
        for window, dilation in DILATED_PATTERNS:
            o_p, lse_p = dilated_attention(q, k, v, window, dilation)
            outs.append(o_p)
            lses.append(lse_p)
        w_mix = jax.nn.softmax(jnp.stack(lses), axis=0)
        attn = jnp.einsum('pbhs,pbhsd->bhsd', w_mix, jnp.stack(outs))
        attn = attn.transpose(0, 2, 1, 3).reshape(B, S, A).astype(x.dtype)

        c0 = 3 * A
        conv = conformer_conv(z[..., c0:c0 + CONV_WIDTH],
                              z[..., c0 + CONV_WIDTH:c0 + 2 * CONV_WIDTH],
                              conv_dw_w, conv_dw_b, conv_ln_g, conv_ln_b)
        mixed = jnp.concatenate([attn, conv.astype(x.dtype)], axis=-1)
        x = x + jnp.einsum('bsc,cd->bsd', mixed, w_out)

        x = x + expert_choice_moe(rms_norm(x, norm2_g), w_router, w1, w3, w2)
    return x
```

```python
import functools

import jax
import jax.numpy as jnp
from jax import lax
from jax.experimental import pallas as pl
from jax.experimental.pallas import tpu as pltpu

F32 = jnp.float32
BF16 = jnp.bfloat16

HEAD_DIM = 64
ATTN_WIDTH = 512
CONV_WIDTH = 512
CONV_KERNEL = 31
CONV_PAD = 16
N_EXPERTS = 16
CAPACITY_FACTOR = 2
ROPE_THETA = 10000.0
EPS = 1e-6
NEG_INF = -1e30
HALF_WINDOW = 64
DILATIONS = (1, 4, 16)
QBLK = 128
VMEM_LIMIT = 56 * 1024 * 1024


def _params(*sem):
    return pltpu.CompilerParams(dimension_semantics=sem, vmem_limit_bytes=VMEM_LIMIT)


def _cast_kernel(w_ref, o_ref):
    o_ref[...] = w_ref[...].astype(BF16)


def _cast_bf16(w, rows_per_block):
    e, r, c = w.shape
    return pl.pallas_call(
        _cast_kernel,
        grid=(e, r // rows_per_block),
        in_specs=[pl.BlockSpec((1, rows_per_block, c), lambda i, j: (i, j, 0))],
        out_specs=pl.BlockSpec((1, rows_per_block, c), lambda i, j: (i, j, 0)),
        out_shape=jax.ShapeDtypeStruct(w.shape, BF16),
        compiler_params=_params("arbitrary", "arbitrary"),
        name="cast_bf16",
    )(w)


def _inproj_kernel(x_ref, g1_ref, win_ref, gq_ref, gk_ref, cos_ref, sin_ref, hsum_ref,
                   q_ref, k_ref, v_ref, u_ref):
    x = x_ref[0]
    ms = jnp.mean(x * x, axis=-1, keepdims=True)
    h = (x * lax.rsqrt(ms + EPS) * g1_ref[...]).astype(BF16)
    z = jnp.dot(h, win_ref[...], preferred_element_type=F32)
    a_w = ATTN_WIDTH
    t = x.shape[0]
    cos = jnp.concatenate([cos_ref[...]] * (a_w // 128), axis=1)
    sin = jnp.concatenate([sin_ref[...]] * (a_w // 128), axis=1)
    lane = lax.broadcasted_iota(jnp.int32, (t, a_w), 1)
    first_half = (lane & (HEAD_DIM - 1)) < (HEAD_DIM // 2)
    hsum = hsum_ref[...]

    def head_norm_rope(tz, g):
        sq = tz * tz
        parts = []
        for c0 in range(0, a_w, 256):
            s = sq[:, c0:c0 + 256]
            hi = s.astype(BF16)
            lo = (s - hi.astype(F32)).astype(BF16)
            parts.append(jnp.dot(hi, hsum, preferred_element_type=F32)
                         + jnp.dot(lo, hsum, preferred_element_type=F32))
        ssum = jnp.concatenate(parts, axis=1)
        y = tz * lax.rsqrt(ssum * (1.0 / HEAD_DIM) + EPS) * g
        partner = jnp.where(first_half, pltpu.roll(y, a_w - HEAD_DIM // 2, 1), pltpu.roll(y, HEAD_DIM // 2, 1))
        return y * cos + partner * sin

    q_ref[0] = head_norm_rope(z[:, 0:a_w], gq_ref[...]) * (HEAD_DIM ** -0.5)
    k_ref[0] = head_norm_rope(z[:, a_w:2 * a_w], gk_ref[...])
    v_ref[0] = z[:, 2 * a_w:3 * a_w]
    c0 = 3 * a_w
    ga = z[:, c0:c0 + CONV_WIDTH]
    gb = z[:, c0 + CONV_WIDTH:c0 + 2 * CONV_WIDTH]
    u_ref[0] = ga / (1.0 + jnp.exp(-gb))


def _inproj(x, norm1_g, w_in, q_norm_g, k_norm_g, tile):
    b, s, d = x.shape
    cols = w_in.shape[1]
    half = HEAD_DIM // 2
    inv_freq = ROPE_THETA ** (-jnp.arange(half, dtype=F32) / half)
    ang = jnp.arange(s, dtype=F32)[:, None] * inv_freq[None, :]
    cos = jnp.tile(jnp.cos(ang), (1, 128 // half))
    sin = jnp.tile(jnp.concatenate([-jnp.sin(ang), jnp.sin(ang)], axis=1), (1, 128 // HEAD_DIM))
    gq = jnp.tile(q_norm_g.astype(F32), ATTN_WIDTH // HEAD_DIM)[None, :]
    gk = jnp.tile(k_norm_g.astype(F32), ATTN_WIDTH // HEAD_DIM)[None, :]
    hid = jnp.arange(256) // HEAD_DIM
    hsum = (hid[:, None] == hid[None, :]).astype(BF16)
    out = jax.ShapeDtypeStruct((b, s, ATTN_WIDTH), F32)
    row_spec = pl.BlockSpec((1, tile, ATTN_WIDTH), lambda i, j: (i, j, 0))
    const = lambda shape: pl.BlockSpec(shape, lambda i, j: (0,) * len(shape))
    return pl.pallas_call(
        _inproj_kernel,
        grid=(b, s // tile),
        in_specs=[pl.BlockSpec((1, tile, d), lambda i, j: (i, j, 0)),
                  const((1, d)), const((d, cols)), const((1, ATTN_WIDTH)), const((1, ATTN_WIDTH)),
                  pl.BlockSpec((tile, 128), lambda i, j: (j, 0)),
                  pl.BlockSpec((tile, 128), lambda i, j: (j, 0)),
                  const((256, 256))],
        out_specs=[row_spec] * 4,
        out_shape=[out] * 4,
        compiler_params=_params("arbitrary", "arbitrary"),
        name="inproj",
    )(x, norm1_g.astype(F32)[None, :], w_in.astype(BF16), gq, gk, cos, sin, hsum)


def _attn_kernel(q_ref, k_ref, v_ref, o_ref, *scratch):
    s_len = q_ref.shape[1]
    o_scr = scratch[0::2]
    l_scr = scratch[1::2]
    lane = lax.broadcasted_iota(jnp.int32, (1, 128), 1)
    head_masks = (lane < HEAD_DIM, lane >= HEAD_DIM)

    def block(q_rows, k_rows, n_keys, delta):
        q = q_ref[0, q_rows, :].astype(BF16)
        k = k_ref[0, k_rows, :]
        v = v_ref[0, k_rows, :]
        rel = (lax.broadcasted_iota(jnp.int32, (QBLK, n_keys), 0)
               - lax.broadcasted_iota(jnp.int32, (QBLK, n_keys), 1) + delta)
        band = jnp.abs(rel) <= HALF_WINDOW
        o_sum = None
        lse_heads = []
        for hm in head_masks:
            kh = jnp.where(hm, k, 0.0).astype(BF16)
            vh = jnp.where(hm, v, 0.0).astype(BF16)
            sc = lax.dot_general(q, kh, (((1,), (1,)), ((), ())), preferred_element_type=F32)
            sc = jnp.where(band, sc, NEG_INF)
            m = jnp.max(sc, axis=1, keepdims=True)
            p = jnp.exp(sc - m)
            l = jnp.sum(p, axis=1, keepdims=True)
            o = jnp.dot(p.astype(BF16), vh, preferred_element_type=F32) / l
            o_sum = o if o_sum is None else o_sum + o
            lse_heads.append(m + jnp.log(l))
        lse = jnp.where(head_masks[0], lse_heads[0], lse_heads[1])
        return o_sum, lse

    for p_idx, dil in enumerate(DILATIONS):
        sub_len = s_len // dil
        n_blocks = sub_len // QBLK
        n_keys = min(2 * QBLK, sub_len)

        def body(i, carry, dil=dil, n_blocks=n_blocks, n_keys=n_keys, sub_len=sub_len, p_idx=p_idx):
            r = i // n_blocks
            j = i % n_blocks
            q0 = j * QBLK
            k0 = jnp.clip(q0 - HALF_WINDOW, 0, sub_len - n_keys)
            if dil == 1:
                q_rows = pl.ds(pl.multiple_of(q0, QBLK), QBLK)
                k_rows = pl.ds(pl.multiple_of(k0, HALF_WINDOW), n_keys)
            else:
                q_rows = pl.ds(r + dil * q0, QBLK, stride=dil)
                k_rows = pl.ds(r + dil * k0, n_keys, stride=dil)
            o, lse = block(q_rows, k_rows, n_keys, q0 - k0)
            o_scr[p_idx][q_rows, :] = o
            l_scr[p_idx][q_rows, :] = lse
            return carry

        lax.fori_loop(0, dil * n_blocks, body, 0)

    lses = [l[...] for l in l_scr]
    mx = jnp.maximum(jnp.maximum(lses[0], lses[1]), lses[2])
    ws = [jnp.exp(l - mx) for l in lses]
    num = ws[0] * o_scr[0][...] + ws[1] * o_scr[1][...] + ws[2] * o_scr[2][...]
    o_ref[0] = num / (ws[0] + ws[1] + ws[2])


def _attention(q, k, v):
    b, s, a_w = q.shape
    spec = pl.BlockSpec((1, s, 128), lambda i, j: (i, 0, j))
    return pl.pallas_call(
        _attn_kernel,
        grid=(b, a_w // 128),
        in_specs=[spec] * 3,
        out_specs=spec,
        out_shape=jax.ShapeDtypeStruct((b, s, a_w), F32),
        scratch_shapes=[pltpu.VMEM((s, 128), F32)] * (2 * len(DILATIONS)),
        compiler_params=_params("arbitrary", "arbitrary"),
        name="dilated_attn",
    )(q, k, v)


def _outproj_kernel(x_ref, attn_ref, u_ref, dww_ref, dwb_ref, lng_ref, lnb_ref, wout_ref, g2_ref, wr_ref,
                    x1_ref, h2_ref, aff_ref, upad_ref, conv_ref):
    j = pl.program_id(1)
    s_len = u_ref.shape[1]
    tile = x_ref.shape[1]
    c_w = CONV_WIDTH
    conv_rows = 256

    @pl.when(j == 0)
    def _conv_whole_sequence():
        upad_ref[0:CONV_PAD, :] = jnp.zeros((CONV_PAD, c_w), F32)
        upad_ref[CONV_PAD + s_len:2 * CONV_PAD + s_len, :] = jnp.zeros((CONV_PAD, c_w), F32)
        upad_ref[CONV_PAD:CONV_PAD + s_len, :] = u_ref[0]
        shift = CONV_PAD - CONV_KERNEL // 2
        for r0 in range(0, s_len, conv_rows):
            acc = jnp.zeros((conv_rows, c_w), F32) + dwb_ref[...]
            for tap in range(CONV_KERNEL):
                acc = acc + dww_ref[tap:tap + 1, :] * upad_ref[r0 + shift + tap:r0 + shift + tap + conv_rows, :]
            mu = jnp.mean(acc, axis=-1, keepdims=True)
            xc = acc - mu
            y = xc * lax.rsqrt(jnp.mean(xc * xc, axis=-1, keepdims=True) + EPS)
            y = y * lng_ref[...] + lnb_ref[...]
            conv_ref[r0:r0 + conv_rows, :] = y / (1.0 + jnp.exp(-y))

    r0 = pl.multiple_of(j * tile, tile)
    mixed = jnp.concatenate([attn_ref[0], conv_ref[pl.ds(r0, tile), :]], axis=1).astype(BF16)
    x1 = x_ref[0] + jnp.dot(mixed, wout_ref[...], preferred_element_type=F32)
    x1_ref[0] = x1
    ms = jnp.mean(x1 * x1, axis=-1, keepdims=True)
    h2 = (x1 * lax.rsqrt(ms + EPS) * g2_ref[...]).astype(BF16)
    h2_ref[0] = h2
    logits = lax.dot_general(wr_ref[...], h2, (((1,), (1,)), ((), ())), preferred_element_type=F32)
    mx = jnp.max(logits, axis=0, keepdims=True)
    ex = jnp.exp(logits - mx)
    aff_ref[0] = ex / jnp.sum(ex, axis=0, keepdims=True)


def _outproj(x, attn, u, conv_dw_w, conv_dw_b, conv_ln_g, conv_ln_b, w_out, norm2_g, w_router, tile):
    b, s, d = x.shape
    const = lambda shape: pl.BlockSpec(shape, lambda i, j: (0,) * len(shape))
    row = lambda width: pl.BlockSpec((1, tile, width), lambda i, j: (i, j, 0))
    vec = lambda a: a.astype(F32)[None, :]
    return pl.pallas_call(
        _outproj_kernel,
        grid=(b, s // tile),
        in_specs=[row(d), row(ATTN_WIDTH),
                  pl.BlockSpec((1, s, CONV_WIDTH), lambda i, j: (i, 0, 0)),
                  const((CONV_KERNEL, CONV_WIDTH)), const((1, CONV_WIDTH)), const((1, CONV_WIDTH)),
                  const((1, CONV_WIDTH)), const((ATTN_WIDTH + CONV_WIDTH, d)), const((1, d)),
                  const((N_EXPERTS, d))],
        out_specs=[row(d), row(d), pl.BlockSpec((1, N_EXPERTS, tile), lambda i, j: (i, 0, j))],
        out_shape=[jax.ShapeDtypeStruct((b, s, d), F32), jax.ShapeDtypeStruct((b, s, d), BF16),
                   jax.ShapeDtypeStruct((b, N_EXPERTS, s), F32)],
        scratch_shapes=[pltpu.VMEM((s + 2 * CONV_PAD, CONV_WIDTH), F32), pltpu.VMEM((s, CONV_WIDTH), F32)],
        compiler_params=_params("arbitrary", "arbitrary"),
        name="outproj",
    )(x, attn, u, conv_dw_w.astype(F32), vec(conv_dw_b), vec(conv_ln_g), vec(conv_ln_b),
      w_out.astype(BF16), vec(norm2_g), w_router.T.astype(BF16))


def _route_kernel(aff_ref, slot_ref, tri_ref, *, capacity):
    rows, s_len = aff_ref.shape

    @pl.when(pl.program_id(0) == 0)
    def _build_prefix_matrix():
        tri_ref[...] = jnp.where(lax.broadcasted_iota(jnp.int32, (s_len, s_len), 0)
                                 <= lax.broadcasted_iota(jnp.int32, (s_len, s_len), 1), 1.0, 0.0).astype(BF16)

    bits = pltpu.bitcast(aff_ref[...], jnp.int32)

    def count(mask):
        return jnp.sum(jnp.where(mask, 1.0, 0.0), axis=1, keepdims=True)

    def search(i, thr):
        cand = thr | jnp.left_shift(jnp.int32(1), 30 - i)
        return jnp.where(count(bits >= cand) >= capacity, cand, thr)

    thr = lax.fori_loop(0, 31, search, jnp.zeros((rows, 1), jnp.int32))
    above = bits > thr
    tied = bits == thr
    need = capacity - count(above)

    def prefix(mask):
        return jnp.dot(jnp.where(mask, 1.0, 0.0).astype(BF16), tri_ref[...], preferred_element_type=F32)

    sel = above | (tied & (prefix(tied) <= need))
    pos = prefix(sel) - 1.0
    slot_ref[...] = jnp.where(sel, pos, -1.0).astype(jnp.int32)


def _route(aff, capacity, rows_per_step):
    r, s = aff.shape
    return pl.pallas_call(
        functools.partial(_route_kernel, capacity=capacity),
        grid=(r // rows_per_step,),
        in_specs=[pl.BlockSpec((rows_per_step, s), lambda i: (i, 0))],
        out_specs=pl.BlockSpec((rows_per_step, s), lambda i: (i, 0)),
        out_shape=jax.ShapeDtypeStruct((r, s), jnp.int32),
        scratch_shapes=[pltpu.VMEM((s, s), BF16)],
        compiler_params=_params("arbitrary"),
        name="route",
    )(aff)


def _ffn_kernel(h_ref, slot_ref, aff_ref, w1_ref, w3_ref, w2_ref, ye_ref, *, capacity):
    s_len = h_ref.shape[1]
    onehot = slot_ref[0, 0] == lax.broadcasted_iota(jnp.int32, (capacity, s_len), 0)
    gate = jnp.sum(jnp.where(onehot, aff_ref[0, 0], 0.0), axis=1, keepdims=True)
    xe = jnp.dot(jnp.where(onehot, 1.0, 0.0).astype(BF16), h_ref[0],
                 preferred_element_type=F32).astype(BF16)
    a = jnp.dot(xe, w1_ref[0], preferred_element_type=F32)
    g = jnp.dot(xe, w3_ref[0], preferred_element_type=F32)
    hid = (a / (1.0 + jnp.exp(-a)) * g).astype(BF16)
    out = jnp.dot(hid, w2_ref[0], preferred_element_type=F32)
    ye_ref[0, 0] = (out * gate).astype(BF16)


def _ffn(h2, slot4, aff4, w1, w3, w2, capacity):
    b, s, d = h2.shape
    e, _, f = w1.shape
    resident = lambda shape: pl.BlockSpec(shape, lambda i, j: (i, 0, 0), pipeline_mode=pl.Buffered(1))
    sel_spec = pl.BlockSpec((1, 1, 1, s), lambda i, j: (j, i, 0, 0))
    return pl.pallas_call(
        functools.partial(_ffn_kernel, capacity=capacity),
        grid=(e, b),
        in_specs=[pl.BlockSpec((1, s, d), lambda i, j: (j, 0, 0)), sel_spec, sel_spec,
                  resident((1, d, f)), resident((1, d, f)), resident((1, f, d))],
        out_specs=pl.BlockSpec((1, 1, capacity, d), lambda i, j: (i, j, 0, 0)),
        out_shape=jax.ShapeDtypeStruct((e, b, capacity, d), BF16),
        compiler_params=_params("arbitrary", "arbitrary"),
        name="expert_ffn",
    )(h2, slot4, aff4, w1, w3, w2)


def _combine_kernel(x1_ref, slot_ref, ye_ref, o_ref, *, capacity):
    tile = x1_ref.shape[1]
    acc = x1_ref[0]
    c_iota = lax.broadcasted_iota(jnp.int32, (capacity, tile), 0)
    for ex in range(N_EXPERTS):
        onehot = jnp.where(slot_ref[0, ex] == c_iota, 1.0, 0.0).astype(BF16)
        acc = acc + lax.dot_general(onehot, ye_ref[ex, 0], (((0,), (0,)), ((), ())),
                                    preferred_element_type=F32)
    o_ref[0] = acc


def _combine(x1, slot4, ye, capacity, tile):
    b, s, d = x1.shape
    return pl.pallas_call(
        functools.partial(_combine_kernel, capacity=capacity),
        grid=(b, s // tile),
        in_specs=[pl.BlockSpec((1, tile, d), lambda i, j: (i, j, 0)),
                  pl.BlockSpec((1, N_EXPERTS, 1, tile), lambda i, j: (i, 0, 0, j)),
                  pl.BlockSpec((N_EXPERTS, 1, capacity, d), lambda i, j: (0, i, 0, 0))],
        out_specs=pl.BlockSpec((1, tile, d), lambda i, j: (i, j, 0)),
        out_shape=jax.ShapeDtypeStruct((b, s, d), F32),
        compiler_params=_params("arbitrary", "arbitrary"),
        name="combine",
    )(x1, slot4, ye)


def kernel(x, norm1_g, w_in, q_norm_g, k_norm_g, conv_dw_w, conv_dw_b, conv_ln_g, conv_ln_b,
           w_out, norm2_g, w_router, w1, w3, w2):
    b, s, d = x.shape
    capacity = CAPACITY_FACTOR * s // N_EXPERTS
    w1b = _cast_bf16(w1, 256)
    w3b = _cast_bf16(w3, 256)
    w2b = _cast_bf16(w2, 256)
    q, k, v, u = _inproj(x, norm1_g, w_in, q_norm_g, k_norm_g, tile=512)
    attn = _attention(q, k, v)
    x1, h2, aff = _outproj(x, attn, u, conv_dw_w, conv_dw_b, conv_ln_g, conv_ln_b,
                           w_out, norm2_g, w_router, tile=512)
    slot = _route(aff.reshape(b * N_EXPERTS, s), capacity, rows_per_step=min(128, b * N_EXPERTS))
    slot4 = slot.reshape(b, N_EXPERTS, 1, s)
    aff4 = aff.reshape(b, N_EXPERTS, 1, s)
    ye = _ffn(h2, slot4, aff4, w1b, w3b, w2b, capacity)
    return _combine(x1, slot4, ye, capacity, tile=512)
```

```python
import functools

import jax
import jax.numpy as jnp
from jax import lax
from jax.experimental import pallas as pl
from jax.experimental.pallas import tpu as pltpu

F32 = jnp.float32
BF16 = jnp.bfloat16

HEAD_DIM = 64
ATTN_WIDTH = 512
CONV_WIDTH = 512
CONV_KERNEL = 31
CONV_PAD = 16
N_EXPERTS = 16
CAPACITY_FACTOR = 2
ROPE_THETA = 10000.0
EPS = 1e-6
NEG_INF = -1e30
HALF_WINDOW = 64
DILATIONS = (1, 4, 16)
QBLK = 128
ATTN_UNROLL = 8
CONV_ROWS = 128
LN_ROWS = 128
DEINT = 4
LANES = 128
LOG2E = 1.4426950408889634
VMEM_LIMIT = 56 * 1024 * 1024


def _params(*sem):
    return pltpu.CompilerParams(dimension_semantics=sem, vmem_limit_bytes=VMEM_LIMIT)


def _cast_kernel(w_ref, o_ref):
    o_ref[...] = w_ref[...].astype(BF16)


def _cast_bf16(w, rows_per_block):
    e, r, c = w.shape
    return pl.pallas_call(
        _cast_kernel,
        grid=(e, r // rows_per_block),
        in_specs=[pl.BlockSpec((1, rows_per_block, c), lambda i, j: (i, j, 0))],
        out_specs=pl.BlockSpec((1, rows_per_block, c), lambda i, j: (i, j, 0)),
        out_shape=jax.ShapeDtypeStruct(w.shape, BF16),
        compiler_params=_params("arbitrary", "arbitrary"),
        name="cast_bf16",
    )(w)


def _inproj_kernel(x_ref, g1_ref, win_ref, gq_ref, gk_ref, cos_ref, sin_ref, hsum_ref,
                   q_ref, k_ref, v_ref, u_ref):
    x = x_ref[0]
    ms = jnp.mean(x * x, axis=-1, keepdims=True)
    h = (x * lax.rsqrt(ms + EPS) * g1_ref[...]).astype(BF16)
    z = jnp.dot(h, win_ref[...], preferred_element_type=F32)
    a_w = ATTN_WIDTH
    t = x.shape[0]
    cos = jnp.concatenate([cos_ref[...]] * (a_w // LANES), axis=1)
    sin = jnp.concatenate([sin_ref[...]] * (a_w // LANES), axis=1)
    lane = lax.broadcasted_iota(jnp.int32, (t, a_w), 1)
    first_half = (lane & (HEAD_DIM - 1)) < (HEAD_DIM // 2)
    hsum = hsum_ref[...]

    def head_norm_rope(tz, g):
        sq = tz * tz
        parts = []
        for c0 in range(0, a_w, 256):
            s = sq[:, c0:c0 + 256]
            hi = s.astype(BF16)
            lo = (s - hi.astype(F32)).astype(BF16)
            parts.append(jnp.dot(hi, hsum, preferred_element_type=F32)
                         + jnp.dot(lo, hsum, preferred_element_type=F32))
        ssum = jnp.concatenate(parts, axis=1)
        y = tz * lax.rsqrt(ssum * (1.0 / HEAD_DIM) + EPS) * g
        partner = jnp.where(first_half, pltpu.roll(y, a_w - HEAD_DIM // 2, 1), pltpu.roll(y, HEAD_DIM // 2, 1))
        return y * cos + partner * sin

    q_ref[0] = head_norm_rope(z[:, 0:a_w], gq_ref[...]) * (HEAD_DIM ** -0.5 * LOG2E)
    k_ref[0] = head_norm_rope(z[:, a_w:2 * a_w], gk_ref[...])
    v_ref[0] = z[:, 2 * a_w:3 * a_w]
    c0 = 3 * a_w
    ga = z[:, c0:c0 + CONV_WIDTH]
    gb = z[:, c0 + CONV_WIDTH:c0 + 2 * CONV_WIDTH]
    u_ref[0] = ga / (1.0 + jnp.exp(-gb))


def _inproj(x, norm1_g, w_in, q_norm_g, k_norm_g, tile):
    b, s, d = x.shape
    cols = w_in.shape[1]
    half = HEAD_DIM // 2
    inv_freq = ROPE_THETA ** (-jnp.arange(half, dtype=F32) / half)
    ang = jnp.arange(s, dtype=F32)[:, None] * inv_freq[None, :]
    cos = jnp.tile(jnp.cos(ang), (1, LANES // half))
    sin = jnp.tile(jnp.concatenate([-jnp.sin(ang), jnp.sin(ang)], axis=1), (1, LANES // HEAD_DIM))
    gq = jnp.tile(q_norm_g.astype(F32), ATTN_WIDTH // HEAD_DIM)[None, :]
    gk = jnp.tile(k_norm_g.astype(F32), ATTN_WIDTH // HEAD_DIM)[None, :]
    hid = jnp.arange(256) // HEAD_DIM
    hsum = (hid[:, None] == hid[None, :]).astype(BF16)
    out = jax.ShapeDtypeStruct((b, s, ATTN_WIDTH), F32)
    row_spec = pl.BlockSpec((1, tile, ATTN_WIDTH), lambda i, j: (i, j, 0))
    const = lambda shape: pl.BlockSpec(shape, lambda i, j: (0,) * len(shape))
    return pl.pallas_call(
        _inproj_kernel,
        grid=(b, s // tile),
        in_specs=[pl.BlockSpec((1, tile, d), lambda i, j: (i, j, 0)),
                  const((1, d)), const((d, cols)), const((1, ATTN_WIDTH)), const((1, ATTN_WIDTH)),
                  pl.BlockSpec((tile, LANES), lambda i, j: (j, 0)),
                  pl.BlockSpec((tile, LANES), lambda i, j: (j, 0)),
                  const((256, 256))],
        out_specs=[row_spec] * 4,
        out_shape=[out] * 4,
        compiler_params=_params("arbitrary", "arbitrary"),
        name="inproj",
    )(x, norm1_g.astype(F32)[None, :], w_in.astype(BF16), gq, gk, cos, sin, hsum)


def _attn_kernel(q_ref, k_ref, v_ref, bias_ref, o_ref, *scratch):
    s_len = q_ref.shape[1]
    sub4 = s_len // DEINT
    qd_ref, kd_ref, vd_ref = scratch[0:3]
    num_scr, den_scr, max_scr = scratch[3::3], scratch[4::3], scratch[5::3]
    lane = lax.broadcasted_iota(jnp.int32, (1, LANES), 1)
    head0 = lane < HEAD_DIM
    nt_dims = (((1,), (1,)), ((), ()))

    for c in range(DEINT):
        rows = pl.ds(c, sub4, stride=DEINT)
        qd_ref[c] = q_ref[0, rows, :]
        kd_ref[c] = k_ref[0, rows, :]
        vd_ref[c] = v_ref[0, rows, :]

    heads = (head0, jnp.logical_not(head0))

    def scores(q, k, n_keys, variant):
        k = k.astype(BF16)
        bias = bias_ref[variant, :, 0:n_keys]
        return [lax.dot_general(jnp.where(hm, q, 0.0).astype(BF16), k, nt_dims, preferred_element_type=F32) + bias
                for hm in heads]

    def weights(sc):
        m = jnp.max(sc, axis=1, keepdims=True)
        return jnp.exp2(sc - m).astype(BF16), m

    def values(pm, v):
        (pa, ma), (pb, mb) = pm
        oa = jnp.dot(pa, jnp.where(heads[0], v, 1.0).astype(BF16), preferred_element_type=F32)
        ob = jnp.dot(pb, jnp.where(heads[1], v, 1.0).astype(BF16), preferred_element_type=F32)
        num = jnp.where(head0, oa, ob)
        den = pltpu.roll(jnp.where(head0, ob, oa), HEAD_DIM, 1)
        return num, den, jnp.where(head0, ma, mb)

    def window(i, sub_len, n_keys):
        n_blocks = sub_len // QBLK
        q0 = (i % n_blocks) * QBLK
        return i // n_blocks, q0, jnp.clip(q0 - HALF_WINDOW, 0, sub_len - n_keys)

    def natural(i):
        _, q0, k0 = window(i, s_len, 2 * QBLK)
        q_rows = pl.ds(pl.multiple_of(q0, QBLK), QBLK)
        k_rows = pl.ds(pl.multiple_of(k0, HALF_WINDOW), 2 * QBLK)
        return ((q_rows,), q_ref[0, q_rows, :], k_ref[0, k_rows, :], v_ref[0, k_rows, :], 2 * QBLK,
                (q0 - k0) // HALF_WINDOW)

    def dilated4(i):
        c, q0, k0 = window(i, sub4, 2 * QBLK)
        q_rows = pl.ds(pl.multiple_of(q0, QBLK), QBLK)
        k_rows = pl.ds(pl.multiple_of(k0, HALF_WINDOW), 2 * QBLK)
        return ((c, q_rows), qd_ref[c, q_rows, :], kd_ref[c, k_rows, :], vd_ref[c, k_rows, :], 2 * QBLK,
                (q0 - k0) // HALF_WINDOW)

    def dilated16(i):
        c, r = i // DEINT, i % DEINT
        rows = pl.ds(r, QBLK, stride=DEINT)
        return (c, rows), qd_ref[c, rows, :], kd_ref[c, rows, :], vd_ref[c, rows, :], QBLK, 0

    for p_idx, (operands, n_total) in enumerate(((natural, s_len // QBLK), (dilated4, s_len // QBLK),
                                                 (dilated16, DEINT * DEINT))):
        def group(g, carry, operands=operands, p_idx=p_idx):
            blocks = [operands(g * ATTN_UNROLL + u) for u in range(ATTN_UNROLL)]
            scs = [scores(q, k, n_keys, variant) for _, q, k, _, n_keys, variant in blocks]
            pms = [[weights(sc) for sc in pair] for pair in scs]
            outs = [values(pm, blk[3]) for pm, blk in zip(pms, blocks)]
            for (idx, *_), (num, den, mx) in zip(blocks, outs):
                num_scr[p_idx][idx + (slice(None),)] = num
                den_scr[p_idx][idx + (slice(None),)] = den
                max_scr[p_idx][idx + (slice(None),)] = mx
            return carry

        lax.fori_loop(0, n_total // ATTN_UNROLL, group, 0)

    def mix(i, carry):
        c = i // (sub4 // QBLK)
        m0 = pl.multiple_of((i % (sub4 // QBLK)) * QBLK, QBLK)
        seq_rows = pl.ds(c + DEINT * m0, QBLK, stride=DEINT)
        idx = ((seq_rows, slice(None)), (c, pl.ds(m0, QBLK), slice(None)), (c, pl.ds(m0, QBLK), slice(None)))
        mxs = [max_scr[p][idx[p]] for p in range(3)]
        top = jnp.maximum(jnp.maximum(mxs[0], mxs[1]), mxs[2])
        ws = [jnp.exp2(m - top) for m in mxs]
        num = sum(ws[p] * num_scr[p][idx[p]] for p in range(3))
        den = sum(ws[p] * den_scr[p][idx[p]] for p in range(3))
        o_ref[0, seq_rows, :] = num / den
        return carry

    lax.fori_loop(0, s_len // QBLK, mix, 0)


def _attention(q, k, v):
    b, s, a_w = q.shape
    assert DILATIONS == (1, DEINT, DEINT * DEINT) and s // DILATIONS[-1] == QBLK
    rel = jnp.arange(QBLK)[:, None] - jnp.arange(2 * QBLK)[None, :]
    bias = jnp.stack([jnp.where(jnp.abs(rel + HALF_WINDOW * var) <= HALF_WINDOW, 0.0, NEG_INF)
                      for var in range(3)]).astype(F32)
    spec = pl.BlockSpec((1, s, LANES), lambda i, j: (i, 0, j))
    return pl.pallas_call(
        _attn_kernel,
        grid=(b, a_w // LANES),
        in_specs=[spec] * 3 + [pl.BlockSpec(bias.shape, lambda i, j: (0, 0, 0))],
        out_specs=spec,
        out_shape=jax.ShapeDtypeStruct((b, s, a_w), F32),
        scratch_shapes=([pltpu.VMEM((DEINT, s // DEINT, LANES), F32)] * 3 + [pltpu.VMEM((s, LANES), F32)] * 3
                        + [pltpu.VMEM((DEINT, s // DEINT, LANES), F32)] * 6),
        compiler_params=_params("arbitrary", "arbitrary"),
        name="dilated_attn",
    )(q, k, v, bias)


def _outproj_kernel(x_ref, attn_ref, u_ref, dww_ref, dwb_ref, lng_ref, lnb_ref, wout_ref, g2_ref, wr_ref,
                    x1_ref, h2_ref, aff_ref, upad_ref, conv_ref):
    j = pl.program_id(1)
    s_len = u_ref.shape[1]
    tile = x_ref.shape[1]
    n_slabs = CONV_WIDTH // LANES
    shift = CONV_PAD - CONV_KERNEL // 2

    def conv_tile(t):
        base = t * tile

        def accumulate(i, carry):
            r0 = (i // n_slabs) * CONV_ROWS
            c = i % n_slabs
            acc = jnp.zeros((CONV_ROWS, LANES), F32) + dwb_ref[c]
            for tap in range(CONV_KERNEL):
                acc = acc + dww_ref[c, tap:tap + 1, :] * upad_ref[c, pl.ds(base + r0 + shift + tap, CONV_ROWS), :]
            conv_ref[c, pl.ds(pl.multiple_of(r0, CONV_ROWS), CONV_ROWS), :] = acc
            return carry

        lax.fori_loop(0, (tile // CONV_ROWS) * n_slabs, accumulate, 0)

        def norm_swish(i, carry):
            rows = pl.ds(pl.multiple_of(i * LN_ROWS, LN_ROWS), LN_ROWS)
            y = jnp.concatenate([conv_ref[c, rows, :] for c in range(n_slabs)], axis=1)
            mu = jnp.mean(y, axis=-1, keepdims=True)
            yc = y - mu
            y = yc * lax.rsqrt(jnp.mean(yc * yc, axis=-1, keepdims=True) + EPS)
            y = y * lng_ref[...] + lnb_ref[...]
            y = y / (1.0 + jnp.exp(-y))
            for c in range(n_slabs):
                conv_ref[c, rows, :] = y[:, c * LANES:(c + 1) * LANES]
            return carry

        lax.fori_loop(0, tile // LN_ROWS, norm_swish, 0)

    @pl.when(j == 0)
    def _start_sequence():
        for c in range(n_slabs):
            upad_ref[c, 0:CONV_PAD, :] = jnp.zeros((CONV_PAD, LANES), F32)
            upad_ref[c, CONV_PAD + s_len:2 * CONV_PAD + s_len, :] = jnp.zeros((CONV_PAD, LANES), F32)
            upad_ref[c, CONV_PAD:CONV_PAD + s_len, :] = u_ref[0, :, c * LANES:(c + 1) * LANES]

    conv_tile(j)
    mixed = jnp.concatenate([attn_ref[0]] + [conv_ref[c] for c in range(n_slabs)], axis=1).astype(BF16)
    x1 = x_ref[0] + jnp.dot(mixed, wout_ref[...], preferred_element_type=F32)
    x1_ref[0] = x1
    ms = jnp.mean(x1 * x1, axis=-1, keepdims=True)
    h2 = (x1 * lax.rsqrt(ms + EPS) * g2_ref[...]).astype(BF16)
    h2_ref[0] = h2
    logits = lax.dot_general(wr_ref[...], h2, (((1,), (1,)), ((), ())), preferred_element_type=F32)
    mx = jnp.max(logits, axis=0, keepdims=True)
    ex = jnp.exp(logits - mx)
    aff_ref[0] = ex / jnp.sum(ex, axis=0, keepdims=True)


def _outproj(x, attn, u, conv_dw_w, conv_dw_b, conv_ln_g, conv_ln_b, w_out, norm2_g, w_router, tile):
    b, s, d = x.shape
    const = lambda shape: pl.BlockSpec(shape, lambda i, j: (0,) * len(shape))
    row = lambda width: pl.BlockSpec((1, tile, width), lambda i, j: (i, j, 0))
    vec = lambda a: a.astype(F32)[None, :]
    n_slabs = CONV_WIDTH // LANES
    slabs = lambda a: a.astype(F32).reshape(-1, n_slabs, LANES).swapaxes(0, 1)
    return pl.pallas_call(
        _outproj_kernel,
        grid=(b, s // tile),
        in_specs=[row(d), row(ATTN_WIDTH),
                  pl.BlockSpec((1, s, CONV_WIDTH), lambda i, j: (i, 0, 0)),
                  const((n_slabs, CONV_KERNEL, LANES)), const((n_slabs, 1, LANES)), const((1, CONV_WIDTH)),
                  const((1, CONV_WIDTH)), const((ATTN_WIDTH + CONV_WIDTH, d)), const((1, d)),
                  const((N_EXPERTS, d))],
        out_specs=[row(d), row(d), pl.BlockSpec((1, N_EXPERTS, tile), lambda i, j: (i, 0, j))],
        out_shape=[jax.ShapeDtypeStruct((b, s, d), F32), jax.ShapeDtypeStruct((b, s, d), BF16),
                   jax.ShapeDtypeStruct((b, N_EXPERTS, s), F32)],
        scratch_shapes=[pltpu.VMEM((n_slabs, s + 2 * CONV_PAD, LANES), F32),
                        pltpu.VMEM((n_slabs, tile, LANES), F32)],
        compiler_params=_params("arbitrary", "arbitrary"),
        name="outproj",
    )(x, attn, u, slabs(conv_dw_w), slabs(conv_dw_b[None, :]), vec(conv_ln_g), vec(conv_ln_b),
      w_out.astype(BF16), vec(norm2_g), w_router.T.astype(BF16))


def _route_kernel(aff_ref, slot_ref, tri_ref, *, capacity):
    rows, s_len = aff_ref.shape

    @pl.when(pl.program_id(0) == 0)
    def _build_prefix_matrix():
        tri_ref[...] = jnp.where(lax.broadcasted_iota(jnp.int32, (s_len, s_len), 0)
                                 <= lax.broadcasted_iota(jnp.int32, (s_len, s_len), 1), 1.0, 0.0).astype(BF16)

    bits = pltpu.bitcast(aff_ref[...], jnp.int32)

    def count(mask):
        return jnp.sum(jnp.where(mask, 1.0, 0.0), axis=1, keepdims=True)

    def search(i, thr):
        cand = thr | jnp.left_shift(jnp.int32(1), 30 - i)
        return jnp.where(count(bits >= cand) >= capacity, cand, thr)

    thr = lax.fori_loop(0, 31, search, jnp.zeros((rows, 1), jnp.int32))
    above = bits > thr
    tied = bits == thr
    need = capacity - count(above)

    def prefix(mask):
        return jnp.dot(jnp.where(mask, 1.0, 0.0).astype(BF16), tri_ref[...], preferred_element_type=F32)

    sel = above | (tied & (prefix(tied) <= need))
    pos = prefix(sel) - 1.0
    slot_ref[...] = jnp.where(sel, pos, -1.0).astype(jnp.int32)


def _route(aff, capacity, rows_per_step):
    r, s = aff.shape
    return pl.pallas_call(
        functools.partial(_route_kernel, capacity=capacity),
        grid=(r // rows_per_step,),
        in_specs=[pl.BlockSpec((rows_per_step, s), lambda i: (i, 0))],
        out_specs=pl.BlockSpec((rows_per_step, s), lambda i: (i, 0)),
        out_shape=jax.ShapeDtypeStruct((r, s), jnp.int32),
        scratch_shapes=[pltpu.VMEM((s, s), BF16)],
        compiler_params=_params("arbitrary"),
        name="route",
    )(aff)


def _ffn_kernel(h_ref, slot_ref, aff_ref, w1_ref, w3_ref, w2_ref, ye_ref, *, capacity):
    s_len = h_ref.shape[1]
    onehot = slot_ref[0, 0] == lax.broadcasted_iota(jnp.int32, (capacity, s_len), 0)
    gate = jnp.sum(jnp.where(onehot, aff_ref[0, 0], 0.0), axis=1, keepdims=True)
    xe = jnp.dot(jnp.where(onehot, 1.0, 0.0).astype(BF16), h_ref[0],
                 preferred_element_type=F32).astype(BF16)
    a = jnp.dot(xe, w1_ref[0], preferred_element_type=F32)
    g = jnp.dot(xe, w3_ref[0], preferred_element_type=F32)
    hid = (a / (1.0 + jnp.exp(-a)) * g).astype(BF16)
    out = jnp.dot(hid, w2_ref[0], preferred_element_type=F32)
    ye_ref[0, 0] = (out * gate).astype(BF16)


def _ffn(h2, slot4, aff4, w1, w3, w2, capacity):
    b, s, d = h2.shape
    e, _, f = w1.shape
    resident = lambda shape: pl.BlockSpec(shape, lambda i, j: (i, 0, 0), pipeline_mode=pl.Buffered(1))
    sel_spec = pl.BlockSpec((1, 1, 1, s), lambda i, j: (j, i, 0, 0))
    return pl.pallas_call(
        functools.partial(_ffn_kernel, capacity=capacity),
        grid=(e, b),
        in_specs=[pl.BlockSpec((1, s, d), lambda i, j: (j, 0, 0)), sel_spec, sel_spec,
                  resident((1, d, f)), resident((1, d, f)), resident((1, f, d))],
        out_specs=pl.BlockSpec((1, 1, capacity, d), lambda i, j: (i, j, 0, 0)),
        out_shape=jax.ShapeDtypeStruct((e, b, capacity, d), BF16),
        compiler_params=_params("arbitrary", "arbitrary"),
        name="expert_ffn",
    )(h2, slot4, aff4, w1, w3, w2)


def _combine_kernel(x1_ref, slot_ref, ye_ref, o_ref, *, capacity):
    tile = x1_ref.shape[1]
    acc = x1_ref[0]
    c_iota = lax.broadcasted_iota(jnp.int32, (capacity, tile), 0)
    for ex in range(N_EXPERTS):
        onehot = jnp.where(slot_ref[0, ex] == c_iota, 1.0, 0.0).astype(BF16)
        acc = acc + lax.dot_general(onehot, ye_ref[ex, 0], (((0,), (0,)), ((), ())),
                                    preferred_element_type=F32)
    o_ref[0] = acc


def _combine(x1, slot4, ye, capacity, tile):
    b, s, d = x1.shape
    return pl.pallas_call(
        functools.partial(_combine_kernel, capacity=capacity),
        grid=(b, s // tile),
        in_specs=[pl.BlockSpec((1, tile, d), lambda i, j: (i, j, 0)),
                  pl.BlockSpec((1, N_EXPERTS, 1, tile), lambda i, j: (i, 0, 0, j)),
                  pl.BlockSpec((N_EXPERTS, 1, capacity, d), lambda i, j: (0, i, 0, 0))],
        out_specs=pl.BlockSpec((1, tile, d), lambda i, j: (i, j, 0)),
        out_shape=jax.ShapeDtypeStruct((b, s, d), F32),
        compiler_params=_params("arbitrary", "arbitrary"),
        name="combine",
    )(x1, slot4, ye)


def kernel(x, norm1_g, w_in, q_norm_g, k_norm_g, conv_dw_w, conv_dw_b, conv_ln_g, conv_ln_b,
           w_out, norm2_g, w_router, w1, w3, w2):
    b, s, d = x.shape
    capacity = CAPACITY_FACTOR * s // N_EXPERTS
    w1b = _cast_bf16(w1, 256)
    w3b = _cast_bf16(w3, 256)
    w2b = _cast_bf16(w2, 256)
    q, k, v, u = _inproj(x, norm1_g, w_in, q_norm_g, k_norm_g, tile=512)
    attn = _attention(q, k, v)
    x1, h2, aff = _outproj(x, attn, u, conv_dw_w, conv_dw_b, conv_ln_g, conv_ln_b,
                           w_out, norm2_g, w_router, tile=512)
    slot = _route(aff.reshape(b * N_EXPERTS, s), capacity, rows_per_step=min(128, b * N_EXPERTS))
    slot4 = slot.reshape(b, N_EXPERTS, 1, s)
    aff4 = aff.reshape(b, N_EXPERTS, 1, s)
    ye = _ffn(h2, slot4, aff4, w1b, w3b, w2b, capacity)
    return _combine(x1, slot4, ye, capacity, tile=512)
```

```python
import functools

import jax
import jax.numpy as jnp
from jax import lax
from jax.experimental import pallas as pl
from jax.experimental.pallas import tpu as pltpu

F32 = jnp.float32
BF16 = jnp.bfloat16

HEAD_DIM = 64
ATTN_WIDTH = 512
CONV_WIDTH = 512
CONV_KERNEL = 31
CONV_PAD = 16
N_EXPERTS = 16
CAPACITY_FACTOR = 2
ROPE_THETA = 10000.0
EPS = 1e-6
NEG_INF = -1e30
HALF_WINDOW = 64
DILATIONS = (1, 4, 16)
QBLK = 128
ATTN_UNROLL = 16
ATTN_SKEW = 4
CONV_ROWS = 128
DEINT = 4
LANES = 128
LOG2E = 1.4426950408889634
VMEM_LIMIT = 56 * 1024 * 1024


def _params(*sem):
    return pltpu.CompilerParams(dimension_semantics=sem, vmem_limit_bytes=VMEM_LIMIT)


def _inproj_kernel(x_ref, g1_ref, win_ref, gq_ref, gk_ref, cos_ref, sin_ref, hsum_ref,
                   q_ref, k_ref, v_ref, u_ref):
    x = x_ref[0]
    ms = jnp.mean(x * x, axis=-1, keepdims=True)
    h = (x * lax.rsqrt(ms + EPS) * g1_ref[...]).astype(BF16)
    a_w = ATTN_WIDTH
    t = x.shape[0]
    cos = jnp.concatenate([cos_ref[...]] * (a_w // LANES), axis=1)
    sin = jnp.concatenate([sin_ref[...]] * (a_w // LANES), axis=1)
    lane = lax.broadcasted_iota(jnp.int32, (t, a_w), 1)
    first_half = (lane & (HEAD_DIM - 1)) < (HEAD_DIM // 2)
    hsum = hsum_ref[...]

    def head_sumsq(tz):
        sq = tz * tz
        return jnp.concatenate([jnp.dot(sq[:, c0:c0 + 256].astype(BF16), hsum, preferred_element_type=F32)
                                for c0 in range(0, a_w, 256)], axis=1)

    def head_norm_rope(tz, ssum, g):
        y = tz * lax.rsqrt(ssum * (1.0 / HEAD_DIM) + EPS) * g
        partner = jnp.where(first_half, pltpu.roll(y, a_w - HEAD_DIM // 2, 1), pltpu.roll(y, HEAD_DIM // 2, 1))
        return y * cos + partner * sin

    zq = jnp.dot(h, win_ref[:, 0:a_w], preferred_element_type=F32)
    zk = jnp.dot(h, win_ref[:, a_w:2 * a_w], preferred_element_type=F32)
    sq_sum, sk_sum = head_sumsq(zq), head_sumsq(zk)
    zr = jnp.dot(h, win_ref[:, 2 * a_w:], preferred_element_type=F32)
    q_ref[0] = head_norm_rope(zq, sq_sum, gq_ref[...]) * (HEAD_DIM ** -0.5 * LOG2E)
    k_ref[0] = head_norm_rope(zk, sk_sum, gk_ref[...])
    v_ref[0] = zr[:, 0:a_w]
    ga = zr[:, a_w:a_w + CONV_WIDTH]
    gb = zr[:, a_w + CONV_WIDTH:a_w + 2 * CONV_WIDTH]
    u_ref[0] = ga / (1.0 + jnp.exp(-gb))


def _inproj(x, norm1_g, w_in, q_norm_g, k_norm_g, tile):
    b, s, d = x.shape
    cols = w_in.shape[1]
    half = HEAD_DIM // 2
    inv_freq = ROPE_THETA ** (-jnp.arange(half, dtype=F32) / half)
    ang = jnp.arange(s, dtype=F32)[:, None] * inv_freq[None, :]
    cos = jnp.tile(jnp.cos(ang), (1, LANES // half))
    sin = jnp.tile(jnp.concatenate([-jnp.sin(ang), jnp.sin(ang)], axis=1), (1, LANES // HEAD_DIM))
    gq = jnp.tile(q_norm_g.astype(F32), ATTN_WIDTH // HEAD_DIM)[None, :]
    gk = jnp.tile(k_norm_g.astype(F32), ATTN_WIDTH // HEAD_DIM)[None, :]
    hid = jnp.arange(256) // HEAD_DIM
    hsum = (hid[:, None] == hid[None, :]).astype(BF16)
    out = jax.ShapeDtypeStruct((b, s, ATTN_WIDTH), F32)
    row_spec = pl.BlockSpec((1, tile, ATTN_WIDTH), lambda i, j: (i, j, 0))
    const = lambda shape: pl.BlockSpec(shape, lambda i, j: (0,) * len(shape))
    return pl.pallas_call(
        _inproj_kernel,
        grid=(b, s // tile),
        in_specs=[pl.BlockSpec((1, tile, d), lambda i, j: (i, j, 0)),
                  const((1, d)), const((d, cols)), const((1, ATTN_WIDTH)), const((1, ATTN_WIDTH)),
                  pl.BlockSpec((tile, LANES), lambda i, j: (j, 0)),
                  pl.BlockSpec((tile, LANES), lambda i, j: (j, 0)),
                  const((256, 256))],
        out_specs=[row_spec] * 4,
        out_shape=[out] * 4,
        compiler_params=_params("arbitrary", "arbitrary"),
        name="inproj",
    )(x, norm1_g.astype(F32)[None, :], w_in.astype(BF16), gq, gk, cos, sin, hsum)


def _attn_kernel(q_ref, k_ref, v_ref, bias_ref, w1_ref, w3_ref, w2_ref,
                 o_ref, w1b_ref, w3b_ref, w2b_ref, *scratch):
    w1b_ref[...] = w1_ref[...].astype(BF16)
    w3b_ref[...] = w3_ref[...].astype(BF16)
    w2b_ref[...] = w2_ref[...].astype(BF16)
    s_len = q_ref.shape[1]
    sub4 = s_len // DEINT
    qd_ref, kd_ref, vd_ref = scratch[0:3]
    num_scr, den_scr, max_scr = scratch[3::3], scratch[4::3], scratch[5::3]
    lane = lax.broadcasted_iota(jnp.int32, (1, LANES), 1)
    head0 = lane < HEAD_DIM
    nt_dims = (((1,), (1,)), ((), ()))

    for c in range(DEINT):
        rows = pl.ds(c, sub4, stride=DEINT)
        qd_ref[c] = q_ref[0, rows, :]
        kd_ref[c] = k_ref[0, rows, :]
        vd_ref[c] = v_ref[0, rows, :]

    heads = (head0, jnp.logical_not(head0))

    def scores(q, k, n_keys, variant):
        k = k.astype(BF16)
        bias = bias_ref[variant, :, 0:n_keys]
        return [lax.dot_general(jnp.where(hm, q, 0.0).astype(BF16), k, nt_dims, preferred_element_type=F32) + bias
                for hm in heads]

    def weights(sc):
        m = jnp.max(sc, axis=1, keepdims=True)
        return jnp.exp2(sc - m).astype(BF16), m

    def values(pm, v):
        (pa, ma), (pb, mb) = pm
        oa = jnp.dot(pa, jnp.where(heads[0], v, 1.0).astype(BF16), preferred_element_type=F32)
        ob = jnp.dot(pb, jnp.where(heads[1], v, 1.0).astype(BF16), preferred_element_type=F32)
        num = jnp.where(head0, oa, ob)
        den = pltpu.roll(jnp.where(head0, ob, oa), HEAD_DIM, 1)
        return num, den, jnp.where(head0, ma, mb)

    def window(i, sub_len, n_keys):
        n_blocks = sub_len // QBLK
        q0 = (i % n_blocks) * QBLK
        return i // n_blocks, q0, jnp.clip(q0 - HALF_WINDOW, 0, sub_len - n_keys)

    def natural(i):
        _, q0, k0 = window(i, s_len, 2 * QBLK)
        q_rows = pl.ds(pl.multiple_of(q0, QBLK), QBLK)
        k_rows = pl.ds(pl.multiple_of(k0, HALF_WINDOW), 2 * QBLK)
        return ((q_rows,), q_ref[0, q_rows, :], k_ref[0, k_rows, :], v_ref[0, k_rows, :], 2 * QBLK,
                (q0 - k0) // HALF_WINDOW)

    def dilated4(i):
        c, q0, k0 = window(i, sub4, 2 * QBLK)
        q_rows = pl.ds(pl.multiple_of(q0, QBLK), QBLK)
        k_rows = pl.ds(pl.multiple_of(k0, HALF_WINDOW), 2 * QBLK)
        return ((c, q_rows), qd_ref[c, q_rows, :], kd_ref[c, k_rows, :], vd_ref[c, k_rows, :], 2 * QBLK,
                (q0 - k0) // HALF_WINDOW)

    def dilated16(i):
        c, r = i // DEINT, i % DEINT
        rows = pl.ds(r, QBLK, stride=DEINT)
        return (c, rows), qd_ref[c, rows, :], kd_ref[c, rows, :], vd_ref[c, rows, :], QBLK, 0

    for p_idx, (operands, n_total) in enumerate(((natural, s_len // QBLK), (dilated4, s_len // QBLK),
                                                 (dilated16, DEINT * DEINT))):
        def group(g, carry, operands=operands, p_idx=p_idx):
            blocks, scs, outs = [], [], []
            for u in range(ATTN_UNROLL + ATTN_SKEW):
                if u < ATTN_UNROLL:
                    blocks.append(operands(g * ATTN_UNROLL + u))
                    _, q, k, _, n_keys, variant = blocks[u]
                    scs.append(scores(q, k, n_keys, variant))
                if u >= ATTN_SKEW:
                    d = u - ATTN_SKEW
                    outs.append(values([weights(sc) for sc in scs[d]], blocks[d][3]))
            for (idx, *_), (num, den, mx) in zip(blocks, outs):
                num_scr[p_idx][idx + (slice(None),)] = num
                den_scr[p_idx][idx + (slice(None),)] = den
                max_scr[p_idx][idx + (slice(None),)] = mx
            return carry

        lax.fori_loop(0, n_total // ATTN_UNROLL, group, 0)

    def mix(i, carry):
        c = i // (sub4 // QBLK)
        m0 = pl.multiple_of((i % (sub4 // QBLK)) * QBLK, QBLK)
        seq_rows = pl.ds(c + DEINT * m0, QBLK, stride=DEINT)
        idx = ((seq_rows, slice(None)), (c, pl.ds(m0, QBLK), slice(None)), (c, pl.ds(m0, QBLK), slice(None)))
        mxs = [max_scr[p][idx[p]] for p in range(3)]
        top = jnp.maximum(jnp.maximum(mxs[0], mxs[1]), mxs[2])
        ws = [jnp.exp2(m - top) for m in mxs]
        num = sum(ws[p] * num_scr[p][idx[p]] for p in range(3))
        den = sum(ws[p] * den_scr[p][idx[p]] for p in range(3))
        o_ref[0, seq_rows, :] = num / den
        return carry

    lax.fori_loop(0, s_len // QBLK, mix, 0)


def _attention(q, k, v, expert_weights):
    b, s, a_w = q.shape
    assert DILATIONS == (1, DEINT, DEINT * DEINT) and s // DILATIONS[-1] == QBLK
    rel = jnp.arange(QBLK)[:, None] - jnp.arange(2 * QBLK)[None, :]
    bias = jnp.stack([jnp.where(jnp.abs(rel + HALF_WINDOW * var) <= HALF_WINDOW, 0.0, NEG_INF)
                      for var in range(3)]).astype(F32)
    n_pairs = a_w // LANES
    n_steps = b * n_pairs
    spec = pl.BlockSpec((1, s, LANES), lambda i, j: (i, 0, j))
    sliced = [w.reshape(n_steps, -1, w.shape[2]) for w in expert_weights]
    w_specs = [pl.BlockSpec((1,) + w.shape[1:], lambda i, j: (i * n_pairs + j, 0, 0)) for w in sliced]
    outs = pl.pallas_call(
        _attn_kernel,
        grid=(b, n_pairs),
        in_specs=[spec] * 3 + [pl.BlockSpec(bias.shape, lambda i, j: (0, 0, 0))] + w_specs,
        out_specs=[spec] + w_specs,
        out_shape=[jax.ShapeDtypeStruct((b, s, a_w), F32)] + [jax.ShapeDtypeStruct(w.shape, BF16) for w in sliced],
        scratch_shapes=([pltpu.VMEM((DEINT, s // DEINT, LANES), F32)] * 3 + [pltpu.VMEM((s, LANES), F32)] * 3
                        + [pltpu.VMEM((DEINT, s // DEINT, LANES), F32)] * 6),
        compiler_params=_params("arbitrary", "arbitrary"),
        name="dilated_attn",
    )(q, k, v, bias, *sliced)
    return outs[0], [o.reshape(w.shape) for o, w in zip(outs[1:], expert_weights)]


def _outproj_kernel(x_ref, attn_ref, u_ref, dww_ref, dwb_ref, lng_ref, lnb_ref, wout_ref, g2_ref, wr_ref,
                    x1_ref, h2_ref, aff_ref, upad_ref, conv_ref):
    j = pl.program_id(1)
    s_len = u_ref.shape[1]
    tile = x_ref.shape[1]
    n_slabs = CONV_WIDTH // LANES
    shift = CONV_PAD - CONV_KERNEL // 2

    def conv_tile(t):
        base = t * tile

        def accumulate(i, carry):
            r0 = (i // n_slabs) * CONV_ROWS
            c = i % n_slabs
            acc = jnp.zeros((CONV_ROWS, LANES), F32) + dwb_ref[c]
            for tap in range(CONV_KERNEL):
                acc = acc + dww_ref[c, tap:tap + 1, :] * upad_ref[c, pl.ds(base + r0 + shift + tap, CONV_ROWS), :]
            conv_ref[c, pl.ds(pl.multiple_of(r0, CONV_ROWS), CONV_ROWS), :] = acc
            return carry

        lax.fori_loop(0, (tile // CONV_ROWS) * n_slabs, accumulate, 0)

    @pl.when(j == 0)
    def _start_sequence():
        for c in range(n_slabs):
            upad_ref[c, 0:CONV_PAD, :] = jnp.zeros((CONV_PAD, LANES), F32)
            upad_ref[c, CONV_PAD + s_len:2 * CONV_PAD + s_len, :] = jnp.zeros((CONV_PAD, LANES), F32)
            upad_ref[c, CONV_PAD:CONV_PAD + s_len, :] = u_ref[0, :, c * LANES:(c + 1) * LANES]

    conv_tile(j)
    y = jnp.concatenate([conv_ref[c] for c in range(n_slabs)], axis=1)
    mu = jnp.mean(y, axis=-1, keepdims=True)
    yc = y - mu
    y = yc * lax.rsqrt(jnp.mean(yc * yc, axis=-1, keepdims=True) + EPS)
    y = y * lng_ref[...] + lnb_ref[...]
    mixed = jnp.concatenate([attn_ref[0], y / (1.0 + jnp.exp(-y))], axis=1).astype(BF16)
    x1 = x_ref[0] + jnp.dot(mixed, wout_ref[...], preferred_element_type=F32)
    x1_ref[0] = x1
    ms = jnp.mean(x1 * x1, axis=-1, keepdims=True)
    h2 = (x1 * lax.rsqrt(ms + EPS) * g2_ref[...]).astype(BF16)
    h2_ref[0] = h2
    logits = lax.dot_general(wr_ref[...], h2, (((1,), (1,)), ((), ())), preferred_element_type=F32)
    mx = jnp.max(logits, axis=0, keepdims=True)
    ex = jnp.exp(logits - mx)
    aff_ref[0] = ex / jnp.sum(ex, axis=0, keepdims=True)


def _outproj(x, attn, u, conv_dw_w, conv_dw_b, conv_ln_g, conv_ln_b, w_out, norm2_g, w_router, tile):
    b, s, d = x.shape
    const = lambda shape: pl.BlockSpec(shape, lambda i, j: (0,) * len(shape))
    row = lambda width: pl.BlockSpec((1, tile, width), lambda i, j: (i, j, 0))
    vec = lambda a: a.astype(F32)[None, :]
    n_slabs = CONV_WIDTH // LANES
    slabs = lambda a: a.astype(F32).reshape(-1, n_slabs, LANES).swapaxes(0, 1)
    return pl.pallas_call(
        _outproj_kernel,
        grid=(b, s // tile),
        in_specs=[row(d), row(ATTN_WIDTH),
                  pl.BlockSpec((1, s, CONV_WIDTH), lambda i, j: (i, 0, 0)),
                  const((n_slabs, CONV_KERNEL, LANES)), const((n_slabs, 1, LANES)), const((1, CONV_WIDTH)),
                  const((1, CONV_WIDTH)), const((ATTN_WIDTH + CONV_WIDTH, d)), const((1, d)),
                  const((N_EXPERTS, d))],
        out_specs=[row(d), row(d), pl.BlockSpec((1, N_EXPERTS, tile), lambda i, j: (i, 0, j))],
        out_shape=[jax.ShapeDtypeStruct((b, s, d), F32), jax.ShapeDtypeStruct((b, s, d), BF16),
                   jax.ShapeDtypeStruct((b, N_EXPERTS, s), F32)],
        scratch_shapes=[pltpu.VMEM((n_slabs, s + 2 * CONV_PAD, LANES), F32),
                        pltpu.VMEM((n_slabs, tile, LANES), F32)],
        compiler_params=_params("arbitrary", "arbitrary"),
        name="outproj",
    )(x, attn, u, slabs(conv_dw_w), slabs(conv_dw_b[None, :]), vec(conv_ln_g), vec(conv_ln_b),
      w_out.astype(BF16), vec(norm2_g), w_router.T.astype(BF16))


def _route_kernel(aff_ref, slot_ref, tri_ref, *, capacity):
    rows, s_len = aff_ref.shape

    @pl.when(pl.program_id(0) == 0)
    def _build_prefix_matrix():
        tri_ref[...] = jnp.where(lax.broadcasted_iota(jnp.int32, (s_len, s_len), 0)
                                 <= lax.broadcasted_iota(jnp.int32, (s_len, s_len), 1), 1.0, 0.0).astype(BF16)

    bits = pltpu.bitcast(aff_ref[...], jnp.int32)

    def count(mask):
        return jnp.sum(jnp.where(mask, 1.0, 0.0), axis=1, keepdims=True)

    def search(i, thr):
        cand = thr | jnp.left_shift(jnp.int32(1), 30 - i)
        return jnp.where(count(bits >= cand) >= capacity, cand, thr)

    thr = lax.fori_loop(0, 31, search, jnp.zeros((rows, 1), jnp.int32))
    above = bits > thr
    tied = bits == thr
    need = capacity - count(above)

    def prefix(mask):
        return jnp.dot(jnp.where(mask, 1.0, 0.0).astype(BF16), tri_ref[...], preferred_element_type=F32)

    sel = above | (tied & (prefix(tied) <= need))
    pos = prefix(sel) - 1.0
    slot_ref[...] = jnp.where(sel, pos, -1.0).astype(jnp.int32)


def _route(aff, capacity, rows_per_step):
    r, s = aff.shape
    return pl.pallas_call(
        functools.partial(_route_kernel, capacity=capacity),
        grid=(r // rows_per_step,),
        in_specs=[pl.BlockSpec((rows_per_step, s), lambda i: (i, 0))],
        out_specs=pl.BlockSpec((rows_per_step, s), lambda i: (i, 0)),
        out_shape=jax.ShapeDtypeStruct((r, s), jnp.int32),
        scratch_shapes=[pltpu.VMEM((s, s), BF16)],
        compiler_params=_params("arbitrary"),
        name="route",
    )(aff)


def _ffn_kernel(h_ref, slot_ref, aff_ref, w1_ref, w3_ref, w2_ref, ye_ref, *, capacity):
    s_len = h_ref.shape[1]
    onehot = slot_ref[0, 0] == lax.broadcasted_iota(jnp.int32, (capacity, s_len), 0)
    gate = jnp.sum(jnp.where(onehot, aff_ref[0, 0], 0.0), axis=1, keepdims=True)
    xe = jnp.dot(jnp.where(onehot, 1.0, 0.0).astype(BF16), h_ref[0],
                 preferred_element_type=F32).astype(BF16)
    a = jnp.dot(xe, w1_ref[0], preferred_element_type=F32)
    g = jnp.dot(xe, w3_ref[0], preferred_element_type=F32)
    hid = (a / (1.0 + jnp.exp(-a)) * g).astype(BF16)
    out = jnp.dot(hid, w2_ref[0], preferred_element_type=F32)
    ye_ref[0, 0] = (out * gate).astype(BF16)


def _ffn(h2, slot4, aff4, w1, w3, w2, capacity):
    b, s, d = h2.shape
    e, _, f = w1.shape
    resident = lambda shape: pl.BlockSpec(shape, lambda i, j: (i, 0, 0))
    sel_spec = pl.BlockSpec((1, 1, 1, s), lambda i, j: (j, i, 0, 0))
    return pl.pallas_call(
        functools.partial(_ffn_kernel, capacity=capacity),
        grid=(e, b),
        in_specs=[pl.BlockSpec((1, s, d), lambda i, j: (j, 0, 0)), sel_spec, sel_spec,
                  resident((1, d, f)), resident((1, d, f)), resident((1, f, d))],
        out_specs=pl.BlockSpec((1, 1, capacity, d), lambda i, j: (i, j, 0, 0)),
        out_shape=jax.ShapeDtypeStruct((e, b, capacity, d), BF16),
        compiler_params=_params("arbitrary", "arbitrary"),
        name="expert_ffn",
    )(h2, slot4, aff4, w1, w3, w2)


def _combine_kernel(x1_ref, slot_ref, ye_ref, o_ref, *, capacity):
    tile = x1_ref.shape[1]
    acc = x1_ref[0]
    c_iota = lax.broadcasted_iota(jnp.int32, (capacity, tile), 0)
    for ex in range(N_EXPERTS):
        onehot = jnp.where(slot_ref[0, ex] == c_iota, 1.0, 0.0).astype(BF16)
        acc = acc + lax.dot_general(onehot, ye_ref[ex, 0], (((0,), (0,)), ((), ())),
                                    preferred_element_type=F32)
    o_ref[0] = acc


def _combine(x1, slot4, ye, capacity, tile):
    b, s, d = x1.shape
    return pl.pallas_call(
        functools.partial(_combine_kernel, capacity=capacity),
        grid=(b, s // tile),
        in_specs=[pl.BlockSpec((1, tile, d), lambda i, j: (i, j, 0)),
                  pl.BlockSpec((1, N_EXPERTS, 1, tile), lambda i, j: (i, 0, 0, j)),
                  pl.BlockSpec((N_EXPERTS, 1, capacity, d), lambda i, j: (0, i, 0, 0))],
        out_specs=pl.BlockSpec((1, tile, d), lambda i, j: (i, j, 0)),
        out_shape=jax.ShapeDtypeStruct((b, s, d), F32),
        compiler_params=_params("arbitrary", "arbitrary"),
        name="combine",
    )(x1, slot4, ye)


def kernel(x, norm1_g, w_in, q_norm_g, k_norm_g, conv_dw_w, conv_dw_b, conv_ln_g, conv_ln_b,
           w_out, norm2_g, w_router, w1, w3, w2):
    b, s, d = x.shape
    capacity = CAPACITY_FACTOR * s // N_EXPERTS
    q, k, v, u = _inproj(x, norm1_g, w_in, q_norm_g, k_norm_g, tile=512)
    attn, (w1b, w3b, w2b) = _attention(q, k, v, (w1, w3, w2))
    x1, h2, aff = _outproj(x, attn, u, conv_dw_w, conv_dw_b, conv_ln_g, conv_ln_b,
                           w_out, norm2_g, w_router, tile=512)
    slot = _route(aff.reshape(b * N_EXPERTS, s), capacity, rows_per_step=min(128, b * N_EXPERTS))
    slot4 = slot.reshape(b, N_EXPERTS, 1, s)
    aff4 = aff.reshape(b, N_EXPERTS, 1, s)
    ye = _ffn(h2, slot4, aff4, w1b, w3b, w2b, capacity)
    return _combine(x1, slot4, ye, capacity, tile=512)
```

```python
import functools

import jax
import jax.numpy as jnp
from jax import lax
from jax.experimental import pallas as pl
from jax.experimental.pallas import tpu as pltpu

F32 = jnp.float32
BF16 = jnp.bfloat16

HEAD_DIM = 64
ATTN_WIDTH = 512
CONV_WIDTH = 512
CONV_KERNEL = 31
CONV_PAD = 16
N_EXPERTS = 16
CAPACITY_FACTOR = 2
ROPE_THETA = 10000.0
EPS = 1e-6
NEG_INF = -1e30
HALF_WINDOW = 64
DILATIONS = (1, 4, 16)
QBLK = 128
ATTN_UNROLL = 16
ATTN_SKEW = 4
CONV_ROWS = 128
DEINT = 4
LANES = 128
MOE_TILE = 512
CUM_STRIDE = 8
SLOT_ALIGN_LOG2 = 4
LOG2E = 1.4426950408889634
VMEM_LIMIT = 56 * 1024 * 1024


def _params(*sem):
    return pltpu.CompilerParams(dimension_semantics=sem, vmem_limit_bytes=VMEM_LIMIT)


def _inproj_kernel(x_ref, g1_ref, win_ref, gq_ref, gk_ref, cos_ref, sin_ref, hsum_ref,
                   q_ref, k_ref, v_ref, u_ref):
    x = x_ref[0]
    ms = jnp.mean(x * x, axis=-1, keepdims=True)
    h = (x * lax.rsqrt(ms + EPS) * g1_ref[...]).astype(BF16)
    a_w = ATTN_WIDTH
    t = x.shape[0]
    cos = jnp.concatenate([cos_ref[...]] * (a_w // LANES), axis=1)
    sin = jnp.concatenate([sin_ref[...]] * (a_w // LANES), axis=1)
    lane = lax.broadcasted_iota(jnp.int32, (t, a_w), 1)
    first_half = (lane & (HEAD_DIM - 1)) < (HEAD_DIM // 2)
    hsum = hsum_ref[...]

    def head_sumsq(tz):
        sq = tz * tz
        return jnp.concatenate([jnp.dot(sq[:, c0:c0 + 256].astype(BF16), hsum, preferred_element_type=F32)
                                for c0 in range(0, a_w, 256)], axis=1)

    def head_norm_rope(tz, ssum, g):
        y = tz * lax.rsqrt(ssum * (1.0 / HEAD_DIM) + EPS) * g
        partner = jnp.where(first_half, pltpu.roll(y, a_w - HEAD_DIM // 2, 1), pltpu.roll(y, HEAD_DIM // 2, 1))
        return y * cos + partner * sin

    zq = jnp.dot(h, win_ref[:, 0:a_w], preferred_element_type=F32)
    zk = jnp.dot(h, win_ref[:, a_w:2 * a_w], preferred_element_type=F32)
    sq_sum, sk_sum = head_sumsq(zq), head_sumsq(zk)
    zr = jnp.dot(h, win_ref[:, 2 * a_w:], preferred_element_type=F32)
    q_ref[0] = head_norm_rope(zq, sq_sum, gq_ref[...]) * (HEAD_DIM ** -0.5 * LOG2E)
    k_ref[0] = head_norm_rope(zk, sk_sum, gk_ref[...])
    v_ref[0] = zr[:, 0:a_w]
    ga = zr[:, a_w:a_w + CONV_WIDTH]
    gb = zr[:, a_w + CONV_WIDTH:a_w + 2 * CONV_WIDTH]
    u_ref[0] = ga / (1.0 + jnp.exp(-gb))


def _inproj(x, norm1_g, w_in, q_norm_g, k_norm_g, tile):
    b, s, d = x.shape
    cols = w_in.shape[1]
    half = HEAD_DIM // 2
    inv_freq = ROPE_THETA ** (-jnp.arange(half, dtype=F32) / half)
    ang = jnp.arange(s, dtype=F32)[:, None] * inv_freq[None, :]
    cos = jnp.tile(jnp.cos(ang), (1, LANES // half))
    sin = jnp.tile(jnp.concatenate([-jnp.sin(ang), jnp.sin(ang)], axis=1), (1, LANES // HEAD_DIM))
    gq = jnp.tile(q_norm_g.astype(F32), ATTN_WIDTH // HEAD_DIM)[None, :]
    gk = jnp.tile(k_norm_g.astype(F32), ATTN_WIDTH // HEAD_DIM)[None, :]
    hid = jnp.arange(256) // HEAD_DIM
    hsum = (hid[:, None] == hid[None, :]).astype(BF16)
    out = jax.ShapeDtypeStruct((b, s, ATTN_WIDTH), F32)
    row_spec = pl.BlockSpec((1, tile, ATTN_WIDTH), lambda i, j: (i, j, 0))
    const = lambda shape: pl.BlockSpec(shape, lambda i, j: (0,) * len(shape))
    return pl.pallas_call(
        _inproj_kernel,
        grid=(b, s // tile),
        in_specs=[pl.BlockSpec((1, tile, d), lambda i, j: (i, j, 0)),
                  const((1, d)), const((d, cols)), const((1, ATTN_WIDTH)), const((1, ATTN_WIDTH)),
                  pl.BlockSpec((tile, LANES), lambda i, j: (j, 0)),
                  pl.BlockSpec((tile, LANES), lambda i, j: (j, 0)),
                  const((256, 256))],
        out_specs=[row_spec] * 4,
        out_shape=[out] * 4,
        compiler_params=_params("arbitrary", "arbitrary"),
        name="inproj",
    )(x, norm1_g.astype(F32)[None, :], w_in.astype(BF16), gq, gk, cos, sin, hsum)


def _attn_kernel(q_ref, k_ref, v_ref, bias_ref, w1_ref, w3_ref, w2_ref,
                 o_ref, w1b_ref, w3b_ref, w2b_ref, *scratch):
    w1b_ref[...] = w1_ref[...].astype(BF16)
    w3b_ref[...] = w3_ref[...].astype(BF16)
    w2b_ref[...] = w2_ref[...].astype(BF16)
    s_len = q_ref.shape[1]
    sub4 = s_len // DEINT
    qd_ref, kd_ref, vd_ref = scratch[0:3]
    num_scr, den_scr, max_scr = scratch[3::3], scratch[4::3], scratch[5::3]
    lane = lax.broadcasted_iota(jnp.int32, (1, LANES), 1)
    head0 = lane < HEAD_DIM
    nt_dims = (((1,), (1,)), ((), ()))

    for c in range(DEINT):
        rows = pl.ds(c, sub4, stride=DEINT)
        qd_ref[c] = q_ref[0, rows, :]
        kd_ref[c] = k_ref[0, rows, :]
        vd_ref[c] = v_ref[0, rows, :]

    heads = (head0, jnp.logical_not(head0))

    def scores(q, k, n_keys, variant):
        k = k.astype(BF16)
        bias = bias_ref[variant, :, 0:n_keys]
        return [lax.dot_general(jnp.where(hm, q, 0.0).astype(BF16), k, nt_dims, preferred_element_type=F32) + bias
                for hm in heads]

    def weights(sc):
        m = jnp.max(sc, axis=1, keepdims=True)
        return jnp.exp2(sc - m).astype(BF16), m

    def values(pm, v):
        (pa, ma), (pb, mb) = pm
        oa = jnp.dot(pa, jnp.where(heads[0], v, 1.0).astype(BF16), preferred_element_type=F32)
        ob = jnp.dot(pb, jnp.where(heads[1], v, 1.0).astype(BF16), preferred_element_type=F32)
        num = jnp.where(head0, oa, ob)
        den = pltpu.roll(jnp.where(head0, ob, oa), HEAD_DIM, 1)
        return num, den, jnp.where(head0, ma, mb)

    def window(i, sub_len, n_keys):
        n_blocks = sub_len // QBLK
        q0 = (i % n_blocks) * QBLK
        return i // n_blocks, q0, jnp.clip(q0 - HALF_WINDOW, 0, sub_len - n_keys)

    def natural(i):
        _, q0, k0 = window(i, s_len, 2 * QBLK)
        q_rows = pl.ds(pl.multiple_of(q0, QBLK), QBLK)
        k_rows = pl.ds(pl.multiple_of(k0, HALF_WINDOW), 2 * QBLK)
        return ((q_rows,), q_ref[0, q_rows, :], k_ref[0, k_rows, :], v_ref[0, k_rows, :], 2 * QBLK,
                (q0 - k0) // HALF_WINDOW)

    def dilated4(i):
        c, q0, k0 = window(i, sub4, 2 * QBLK)
        q_rows = pl.ds(pl.multiple_of(q0, QBLK), QBLK)
        k_rows = pl.ds(pl.multiple_of(k0, HALF_WINDOW), 2 * QBLK)
        return ((c, q_rows), qd_ref[c, q_rows, :], kd_ref[c, k_rows, :], vd_ref[c, k_rows, :], 2 * QBLK,
                (q0 - k0) // HALF_WINDOW)

    def dilated16(i):
        c, r = i // DEINT, i % DEINT
        rows = pl.ds(r, QBLK, stride=DEINT)
        return (c, rows), qd_ref[c, rows, :], kd_ref[c, rows, :], vd_ref[c, rows, :], QBLK, 0

    for p_idx, (operands, n_total) in enumerate(((natural, s_len // QBLK), (dilated4, s_len // QBLK),
                                                 (dilated16, DEINT * DEINT))):
        def group(g, carry, operands=operands, p_idx=p_idx):
            blocks, scs, outs = [], [], []
            for u in range(ATTN_UNROLL + ATTN_SKEW):
                if u < ATTN_UNROLL:
                    blocks.append(operands(g * ATTN_UNROLL + u))
                    _, q, k, _, n_keys, variant = blocks[u]
                    scs.append(scores(q, k, n_keys, variant))
                if u >= ATTN_SKEW:
                    d = u - ATTN_SKEW
                    outs.append(values([weights(sc) for sc in scs[d]], blocks[d][3]))
            for (idx, *_), (num, den, mx) in zip(blocks, outs):
                num_scr[p_idx][idx + (slice(None),)] = num
                den_scr[p_idx][idx + (slice(None),)] = den
                max_scr[p_idx][idx + (slice(None),)] = mx
            return carry

        lax.fori_loop(0, n_total // ATTN_UNROLL, group, 0)

    def mix(i, carry):
        c = i // (sub4 // QBLK)
        m0 = pl.multiple_of((i % (sub4 // QBLK)) * QBLK, QBLK)
        seq_rows = pl.ds(c + DEINT * m0, QBLK, stride=DEINT)
        idx = ((seq_rows, slice(None)), (c, pl.ds(m0, QBLK), slice(None)), (c, pl.ds(m0, QBLK), slice(None)))
        mxs = [max_scr[p][idx[p]] for p in range(3)]
        top = jnp.maximum(jnp.maximum(mxs[0], mxs[1]), mxs[2])
        ws = [jnp.exp2(m - top) for m in mxs]
        num = sum(ws[p] * num_scr[p][idx[p]] for p in range(3))
        den = sum(ws[p] * den_scr[p][idx[p]] for p in range(3))
        o_ref[0, seq_rows, :] = num / den
        return carry

    lax.fori_loop(0, s_len // QBLK, mix, 0)


def _attention(q, k, v, expert_weights):
    b, s, a_w = q.shape
    assert DILATIONS == (1, DEINT, DEINT * DEINT) and s // DILATIONS[-1] == QBLK
    rel = jnp.arange(QBLK)[:, None] - jnp.arange(2 * QBLK)[None, :]
    bias = jnp.stack([jnp.where(jnp.abs(rel + HALF_WINDOW * var) <= HALF_WINDOW, 0.0, NEG_INF)
                      for var in range(3)]).astype(F32)
    n_pairs = a_w // LANES
    n_steps = b * n_pairs
    spec = pl.BlockSpec((1, s, LANES), lambda i, j: (i, 0, j))
    sliced = [w.reshape(n_steps, -1, w.shape[2]) for w in expert_weights]
    w_specs = [pl.BlockSpec((1,) + w.shape[1:], lambda i, j: (i * n_pairs + j, 0, 0)) for w in sliced]
    outs = pl.pallas_call(
        _attn_kernel,
        grid=(b, n_pairs),
        in_specs=[spec] * 3 + [pl.BlockSpec(bias.shape, lambda i, j: (0, 0, 0))] + w_specs,
        out_specs=[spec] + w_specs,
        out_shape=[jax.ShapeDtypeStruct((b, s, a_w), F32)] + [jax.ShapeDtypeStruct(w.shape, BF16) for w in sliced],
        scratch_shapes=([pltpu.VMEM((DEINT, s // DEINT, LANES), F32)] * 3 + [pltpu.VMEM((s, LANES), F32)] * 3
                        + [pltpu.VMEM((DEINT, s // DEINT, LANES), F32)] * 6),
        compiler_params=_params("arbitrary", "arbitrary"),
        name="dilated_attn",
    )(q, k, v, bias, *sliced)
    return outs[0], [o.reshape(w.shape) for o, w in zip(outs[1:], expert_weights)]


def _outproj_kernel(x_ref, attn_ref, u_ref, dww_ref, dwb_ref, lng_ref, lnb_ref, wout_ref, g2_ref, wr_ref,
                    x1_ref, h2_ref, aff_ref, upad_ref, conv_ref):
    j = pl.program_id(1)
    s_len = u_ref.shape[1]
    tile = x_ref.shape[1]
    n_slabs = CONV_WIDTH // LANES
    shift = CONV_PAD - CONV_KERNEL // 2

    def conv_tile(t):
        base = t * tile

        def accumulate(i, carry):
            r0 = (i // n_slabs) * CONV_ROWS
            c = i % n_slabs
            acc = jnp.zeros((CONV_ROWS, LANES), F32) + dwb_ref[c]
            for tap in range(CONV_KERNEL):
                acc = acc + dww_ref[c, tap:tap + 1, :] * upad_ref[c, pl.ds(base + r0 + shift + tap, CONV_ROWS), :]
            conv_ref[c, pl.ds(pl.multiple_of(r0, CONV_ROWS), CONV_ROWS), :] = acc
            return carry

        lax.fori_loop(0, (tile // CONV_ROWS) * n_slabs, accumulate, 0)

    @pl.when(j == 0)
    def _start_sequence():
        for c in range(n_slabs):
            upad_ref[c, 0:CONV_PAD, :] = jnp.zeros((CONV_PAD, LANES), F32)
            upad_ref[c, CONV_PAD + s_len:2 * CONV_PAD + s_len, :] = jnp.zeros((CONV_PAD, LANES), F32)
            upad_ref[c, CONV_PAD:CONV_PAD + s_len, :] = u_ref[0, :, c * LANES:(c + 1) * LANES]

    conv_tile(j)
    y = jnp.concatenate([conv_ref[c] for c in range(n_slabs)], axis=1)
    mu = jnp.mean(y, axis=-1, keepdims=True)
    yc = y - mu
    y = yc * lax.rsqrt(jnp.mean(yc * yc, axis=-1, keepdims=True) + EPS)
    y = y * lng_ref[...] + lnb_ref[...]
    mixed = jnp.concatenate([attn_ref[0], y / (1.0 + jnp.exp(-y))], axis=1).astype(BF16)
    x1 = x_ref[0] + jnp.dot(mixed, wout_ref[...], preferred_element_type=F32)
    x1_ref[0] = x1
    ms = jnp.mean(x1 * x1, axis=-1, keepdims=True)
    h2 = (x1 * lax.rsqrt(ms + EPS) * g2_ref[...]).astype(BF16)
    h2_ref[0] = h2
    logits = lax.dot_general(wr_ref[...], h2, (((1,), (1,)), ((), ())), preferred_element_type=F32)
    mx = jnp.max(logits, axis=0, keepdims=True)
    ex = jnp.exp(logits - mx)
    aff_ref[0] = ex / jnp.sum(ex, axis=0, keepdims=True)


def _outproj(x, attn, u, conv_dw_w, conv_dw_b, conv_ln_g, conv_ln_b, w_out, norm2_g, w_router, tile):
    b, s, d = x.shape
    const = lambda shape: pl.BlockSpec(shape, lambda i, j: (0,) * len(shape))
    row = lambda width: pl.BlockSpec((1, tile, width), lambda i, j: (i, j, 0))
    vec = lambda a: a.astype(F32)[None, :]
    n_slabs = CONV_WIDTH // LANES
    slabs = lambda a: a.astype(F32).reshape(-1, n_slabs, LANES).swapaxes(0, 1)
    return pl.pallas_call(
        _outproj_kernel,
        grid=(b, s // tile),
        in_specs=[row(d), row(ATTN_WIDTH),
                  pl.BlockSpec((1, s, CONV_WIDTH), lambda i, j: (i, 0, 0)),
                  const((n_slabs, CONV_KERNEL, LANES)), const((n_slabs, 1, LANES)), const((1, CONV_WIDTH)),
                  const((1, CONV_WIDTH)), const((ATTN_WIDTH + CONV_WIDTH, d)), const((1, d)),
                  const((N_EXPERTS, d))],
        out_specs=[row(d), row(d), pl.BlockSpec((1, N_EXPERTS, tile), lambda i, j: (i, 0, j))],
        out_shape=[jax.ShapeDtypeStruct((b, s, d), F32), jax.ShapeDtypeStruct((b, s, d), BF16),
                   jax.ShapeDtypeStruct((b, N_EXPERTS, s), F32)],
        scratch_shapes=[pltpu.VMEM((n_slabs, s + 2 * CONV_PAD, LANES), F32),
                        pltpu.VMEM((n_slabs, tile, LANES), F32)],
        compiler_params=_params("arbitrary", "arbitrary"),
        name="outproj",
    )(x, attn, u, slabs(conv_dw_w), slabs(conv_dw_b[None, :]), vec(conv_ln_g), vec(conv_ln_b),
      w_out.astype(BF16), vec(norm2_g), w_router.T.astype(BF16))


def _route_kernel(aff_ref, slot_ref, cum_ref, tri_ref, *, capacity, tile):
    rows, s_len = aff_ref.shape

    @pl.when(pl.program_id(0) == 0)
    def _build_prefix_matrix():
        tri_ref[...] = jnp.where(lax.broadcasted_iota(jnp.int32, (s_len, s_len), 0)
                                 <= lax.broadcasted_iota(jnp.int32, (s_len, s_len), 1), 1.0, 0.0).astype(BF16)

    bits = pltpu.bitcast(aff_ref[...], jnp.int32)

    def count(mask):
        return jnp.sum(jnp.where(mask, 1.0, 0.0), axis=1, keepdims=True)

    def search(i, thr):
        cand = thr | jnp.left_shift(jnp.int32(1), 30 - i)
        return jnp.where(count(bits >= cand) >= capacity, cand, thr)

    thr = lax.fori_loop(0, 31, search, jnp.zeros((rows, 1), jnp.int32))
    above = bits > thr
    tied = bits == thr
    need = capacity - count(above)

    def prefix(mask):
        return jnp.dot(jnp.where(mask, 1.0, 0.0).astype(BF16), tri_ref[...], preferred_element_type=F32)

    sel = above | (tied & (prefix(tied) <= need))
    sel_bf16 = jnp.where(sel, 1.0, 0.0).astype(BF16)
    pos = jnp.dot(sel_bf16, tri_ref[...], preferred_element_type=F32) - 1.0
    slot_ref[...] = jnp.where(sel, pos, -1.0).astype(jnp.int32)
    before = jnp.where(lax.broadcasted_iota(jnp.int32, (s_len, LANES), 0)
                       < lax.broadcasted_iota(jnp.int32, (s_len, LANES), 1) * tile, 1.0, 0.0).astype(BF16)
    cum_ref[...] = jnp.dot(sel_bf16, before, preferred_element_type=F32).astype(jnp.int32)


def _route(aff, capacity, rows_per_step, tile):
    r, s = aff.shape
    return pl.pallas_call(
        functools.partial(_route_kernel, capacity=capacity, tile=tile),
        grid=(r // rows_per_step,),
        in_specs=[pl.BlockSpec((rows_per_step, s), lambda i: (i, 0))],
        out_specs=[pl.BlockSpec((rows_per_step, s), lambda i: (i, 0)),
                   pl.BlockSpec((rows_per_step, LANES), lambda i: (i, 0))],
        out_shape=[jax.ShapeDtypeStruct((r, s), jnp.int32), jax.ShapeDtypeStruct((r, LANES), jnp.int32)],
        scratch_shapes=[pltpu.VMEM((s, s), BF16)],
        compiler_params=_params("arbitrary"),
        name="route",
    )(aff)


def _slot_window(cum_ref, row, t, capacity):
    win = capacity // 2
    first = cum_ref[row * CUM_STRIDE + t]
    start = jnp.minimum(lax.shift_left(lax.shift_right_logical(first, SLOT_ALIGN_LOG2), SLOT_ALIGN_LOG2),
                        capacity - win)
    return start, cum_ref[row * CUM_STRIDE + t + 1] <= start + win


def _ffn_kernel(cum_ref, h_ref, slot_ref, aff_ref, w1_ref, w3_ref, w2_ref, ye_ref, xe_ref, *, capacity, tile):
    i, j = pl.program_id(0), pl.program_id(1)
    s_len = h_ref.shape[1]
    win = capacity // 2
    windows = [_slot_window(cum_ref, j * pl.num_programs(0) + i, t, capacity) for t in range(s_len // tile)]
    all_fit = functools.reduce(jnp.logical_and, [fit for _, fit in windows])
    onehot = slot_ref[0, 0] == lax.broadcasted_iota(jnp.int32, (capacity, s_len), 0)
    gate = jnp.sum(jnp.where(onehot, aff_ref[0, 0], 0.0), axis=1, keepdims=True)

    def expert(xe):
        a = jnp.dot(xe, w1_ref[0], preferred_element_type=F32)
        g = jnp.dot(xe, w3_ref[0], preferred_element_type=F32)
        hid = (a / (1.0 + jnp.exp(-a)) * g).astype(BF16)
        out = jnp.dot(hid, w2_ref[0], preferred_element_type=F32)
        return (out * gate).astype(BF16)

    xe_ref[...] = jnp.zeros(xe_ref.shape, F32)
    c_iota = lax.broadcasted_iota(jnp.int32, (win, tile), 0)
    for t, (start, _) in enumerate(windows):
        start = pl.multiple_of(start, 1 << SLOT_ALIGN_LOG2)
        cols = slice(t * tile, (t + 1) * tile)
        hot = jnp.where(slot_ref[0, 0, :, cols] - start == c_iota, 1.0, 0.0).astype(BF16)
        xe_ref[pl.ds(start, win), :] += jnp.dot(hot, h_ref[0, cols, :], preferred_element_type=F32)
    ye_ref[0, 0] = expert(xe_ref[...].astype(BF16))

    @pl.when(jnp.logical_not(all_fit))
    def _redo_with_full_gather():
        xe = jnp.dot(jnp.where(onehot, 1.0, 0.0).astype(BF16), h_ref[0], preferred_element_type=F32)
        ye_ref[0, 0] = expert(xe.astype(BF16))


def _ffn(cum_table, h2, slot4, aff4, w1, w3, w2, capacity, tile):
    b, s, d = h2.shape
    e, _, f = w1.shape
    resident = lambda shape: pl.BlockSpec(shape, lambda i, j, cum: (i, 0, 0))
    sel_spec = pl.BlockSpec((1, 1, 1, s), lambda i, j, cum: (j, i, 0, 0))
    return pl.pallas_call(
        functools.partial(_ffn_kernel, capacity=capacity, tile=tile),
        grid_spec=pltpu.PrefetchScalarGridSpec(
            num_scalar_prefetch=1,
            grid=(e, b),
            in_specs=[pl.BlockSpec((1, s, d), lambda i, j, cum: (j, 0, 0)), sel_spec, sel_spec,
                      resident((1, d, f)), resident((1, d, f)), resident((1, f, d))],
            out_specs=pl.BlockSpec((1, 1, capacity, d), lambda i, j, cum: (i, j, 0, 0)),
            scratch_shapes=[pltpu.VMEM((capacity, d), F32)]),
        out_shape=jax.ShapeDtypeStruct((e, b, capacity, d), BF16),
        compiler_params=_params("arbitrary", "arbitrary"),
        name="expert_ffn",
    )(cum_table, h2, slot4, aff4, w1, w3, w2)


def _combine_kernel(cum_ref, x1_ref, slot_ref, ye_ref, o_ref, *, capacity):
    i, j = pl.program_id(0), pl.program_id(1)
    tile = x1_ref.shape[1]
    win = capacity // 2
    windows = [_slot_window(cum_ref, i * N_EXPERTS + ex, j, capacity) for ex in range(N_EXPERTS)]
    all_fit = functools.reduce(jnp.logical_and, [fit for _, fit in windows])
    contract_rows = (((0,), (0,)), ((), ()))

    @pl.when(all_fit)
    def _paired_windows():
        acc = x1_ref[0]
        c_iota = lax.broadcasted_iota(jnp.int32, (win, tile), 0)
        for e0 in range(0, N_EXPERTS, 2):
            hots, rows = [], []
            for ex in (e0, e0 + 1):
                start = pl.multiple_of(windows[ex][0], 1 << SLOT_ALIGN_LOG2)
                hots.append(jnp.where(slot_ref[0, ex] - start == c_iota, 1.0, 0.0).astype(BF16))
                rows.append(ye_ref[ex, 0, pl.ds(start, win), :])
            acc = acc + lax.dot_general(jnp.concatenate(hots, axis=0), jnp.concatenate(rows, axis=0),
                                        contract_rows, preferred_element_type=F32)
        o_ref[0] = acc

    @pl.when(jnp.logical_not(all_fit))
    def _full():
        acc = x1_ref[0]
        c_iota = lax.broadcasted_iota(jnp.int32, (capacity, tile), 0)
        for ex in range(N_EXPERTS):
            onehot = jnp.where(slot_ref[0, ex] == c_iota, 1.0, 0.0).astype(BF16)
            acc = acc + lax.dot_general(onehot, ye_ref[ex, 0], contract_rows, preferred_element_type=F32)
        o_ref[0] = acc


def _combine(cum_table, x1, slot4, ye, capacity, tile):
    b, s, d = x1.shape
    return pl.pallas_call(
        functools.partial(_combine_kernel, capacity=capacity),
        grid_spec=pltpu.PrefetchScalarGridSpec(
            num_scalar_prefetch=1,
            grid=(b, s // tile),
            in_specs=[pl.BlockSpec((1, tile, d), lambda i, j, cum: (i, j, 0)),
                      pl.BlockSpec((1, N_EXPERTS, 1, tile), lambda i, j, cum: (i, 0, 0, j)),
                      pl.BlockSpec((N_EXPERTS, 1, capacity, d), lambda i, j, cum: (0, i, 0, 0))],
            out_specs=pl.BlockSpec((1, tile, d), lambda i, j, cum: (i, j, 0))),
        out_shape=jax.ShapeDtypeStruct((b, s, d), F32),
        compiler_params=_params("arbitrary", "arbitrary"),
        name="combine",
    )(cum_table, x1, slot4, ye)


def kernel(x, norm1_g, w_in, q_norm_g, k_norm_g, conv_dw_w, conv_dw_b, conv_ln_g, conv_ln_b,
           w_out, norm2_g, w_router, w1, w3, w2):
    b, s, d = x.shape
    capacity = CAPACITY_FACTOR * s // N_EXPERTS
    q, k, v, u = _inproj(x, norm1_g, w_in, q_norm_g, k_norm_g, tile=512)
    attn, (w1b, w3b, w2b) = _attention(q, k, v, (w1, w3, w2))
    x1, h2, aff = _outproj(x, attn, u, conv_dw_w, conv_dw_b, conv_ln_g, conv_ln_b,
                           w_out, norm2_g, w_router, tile=512)
    slot, cum = _route(aff.reshape(b * N_EXPERTS, s), capacity, rows_per_step=min(128, b * N_EXPERTS),
                       tile=MOE_TILE)
    assert s // MOE_TILE < CUM_STRIDE
    cum_table = cum[:, :CUM_STRIDE].reshape(-1)
    slot4 = slot.reshape(b, N_EXPERTS, 1, s)
    aff4 = aff.reshape(b, N_EXPERTS, 1, s)
    ye = _ffn(cum_table, h2, slot4, aff4, w1b, w3b, w2b, capacity, tile=MOE_TILE)
    return _combine(cum_table, x1, slot4, ye, capacity, tile=MOE_TILE)
```

```python
import functools

import jax
import jax.numpy as jnp
from jax import lax
from jax.experimental import pallas as pl
from jax.experimental.pallas import tpu as pltpu

F32 = jnp.float32
BF16 = jnp.bfloat16

HEAD_DIM = 64
ATTN_WIDTH = 512
CONV_WIDTH = 512
CONV_KERNEL = 31
CONV_PAD = 16
N_EXPERTS = 16
CAPACITY_FACTOR = 2
ROPE_THETA = 10000.0
EPS = 1e-6
NEG_INF = -1e30
HALF_WINDOW = 64
DILATIONS = (1, 4, 16)
QBLK = 128
ATTN_UNROLL = 16
ATTN_SKEW = 4
CONV_ROWS = 128
DEINT = 4
LANES = 128
MOE_TILE = 256
WINDOW_DIV = 4
COMBINE_TILE = 512
CUM_STRIDE = 16
SLOT_ALIGN_LOG2 = 4
LOG2E = 1.4426950408889634
VMEM_LIMIT = 56 * 1024 * 1024


def _params(*sem):
    return pltpu.CompilerParams(dimension_semantics=sem, vmem_limit_bytes=VMEM_LIMIT)


def _inproj_kernel(x_ref, g1_ref, win_ref, gq_ref, gk_ref, cos_ref, sin_ref, hsum_ref,
                   q_ref, k_ref, v_ref, u_ref):
    x = x_ref[0]
    ms = jnp.mean(x * x, axis=-1, keepdims=True)
    h = (x * lax.rsqrt(ms + EPS) * g1_ref[...]).astype(BF16)
    a_w = ATTN_WIDTH
    t = x.shape[0]
    cos = jnp.concatenate([cos_ref[...]] * (a_w // LANES), axis=1)
    sin = jnp.concatenate([sin_ref[...]] * (a_w // LANES), axis=1)
    lane = lax.broadcasted_iota(jnp.int32, (t, a_w), 1)
    first_half = (lane & (HEAD_DIM - 1)) < (HEAD_DIM // 2)
    hsum = hsum_ref[...]

    def head_sumsq(tz):
        sq = tz * tz
        return jnp.concatenate([jnp.dot(sq[:, c0:c0 + 256].astype(BF16), hsum, preferred_element_type=F32)
                                for c0 in range(0, a_w, 256)], axis=1)

    def head_norm_rope(tz, ssum, g):
        y = tz * lax.rsqrt(ssum * (1.0 / HEAD_DIM) + EPS) * g
        partner = jnp.where(first_half, pltpu.roll(y, a_w - HEAD_DIM // 2, 1), pltpu.roll(y, HEAD_DIM // 2, 1))
        return y * cos + partner * sin

    zq = jnp.dot(h, win_ref[:, 0:a_w], preferred_element_type=F32)
    zk = jnp.dot(h, win_ref[:, a_w:2 * a_w], preferred_element_type=F32)
    sq_sum, sk_sum = head_sumsq(zq), head_sumsq(zk)
    zr = jnp.dot(h, win_ref[:, 2 * a_w:], preferred_element_type=F32)
    q_ref[0] = head_norm_rope(zq, sq_sum, gq_ref[...]) * (HEAD_DIM ** -0.5 * LOG2E)
    k_ref[0] = head_norm_rope(zk, sk_sum, gk_ref[...])
    v_ref[0] = zr[:, 0:a_w]
    ga = zr[:, a_w:a_w + CONV_WIDTH]
    gb = zr[:, a_w + CONV_WIDTH:a_w + 2 * CONV_WIDTH]
    u_ref[0] = ga / (1.0 + jnp.exp(-gb))


def _inproj(x, norm1_g, w_in, q_norm_g, k_norm_g, tile):
    b, s, d = x.shape
    cols = w_in.shape[1]
    half = HEAD_DIM // 2
    inv_freq = ROPE_THETA ** (-jnp.arange(half, dtype=F32) / half)
    ang = jnp.arange(s, dtype=F32)[:, None] * inv_freq[None, :]
    cos = jnp.tile(jnp.cos(ang), (1, LANES // half))
    sin = jnp.tile(jnp.concatenate([-jnp.sin(ang), jnp.sin(ang)], axis=1), (1, LANES // HEAD_DIM))
    gq = jnp.tile(q_norm_g.astype(F32), ATTN_WIDTH // HEAD_DIM)[None, :]
    gk = jnp.tile(k_norm_g.astype(F32), ATTN_WIDTH // HEAD_DIM)[None, :]
    hid = jnp.arange(256) // HEAD_DIM
    hsum = (hid[:, None] == hid[None, :]).astype(BF16)
    out = jax.ShapeDtypeStruct((b, s, ATTN_WIDTH), F32)
    row_spec = pl.BlockSpec((1, tile, ATTN_WIDTH), lambda i, j: (i, j, 0))
    const = lambda shape: pl.BlockSpec(shape, lambda i, j: (0,) * len(shape))
    return pl.pallas_call(
        _inproj_kernel,
        grid=(b, s // tile),
        in_specs=[pl.BlockSpec((1, tile, d), lambda i, j: (i, j, 0)),
                  const((1, d)), const((d, cols)), const((1, ATTN_WIDTH)), const((1, ATTN_WIDTH)),
                  pl.BlockSpec((tile, LANES), lambda i, j: (j, 0)),
                  pl.BlockSpec((tile, LANES), lambda i, j: (j, 0)),
                  const((256, 256))],
        out_specs=[row_spec] * 4,
        out_shape=[out] * 4,
        compiler_params=_params("arbitrary", "arbitrary"),
        name="inproj",
    )(x, norm1_g.astype(F32)[None, :], w_in.astype(BF16), gq, gk, cos, sin, hsum)


def _attn_kernel(q_ref, k_ref, v_ref, bias_ref, w1_ref, w3_ref, w2_ref,
                 o_ref, w1b_ref, w3b_ref, w2b_ref, *scratch):
    w1b_ref[...] = w1_ref[...].astype(BF16)
    w3b_ref[...] = w3_ref[...].astype(BF16)
    w2b_ref[...] = w2_ref[...].astype(BF16)
    s_len = q_ref.shape[1]
    sub4 = s_len // DEINT
    qd_ref, kd_ref, vd_ref = scratch[0:3]
    wide_scr = scratch[3:6]
    seq_scr = scratch[6:9]
    lane = lax.broadcasted_iota(jnp.int32, (1, LANES), 1)
    head0 = lane < HEAD_DIM
    nt_dims = (((1,), (1,)), ((), ()))

    for c in range(DEINT):
        rows = pl.ds(c, sub4, stride=DEINT)
        qd_ref[c] = q_ref[0, rows, :]
        kd_ref[c] = k_ref[0, rows, :]
        vd_ref[c] = v_ref[0, rows, :]

    heads = (head0, jnp.logical_not(head0))

    def scores(q, k, n_keys, variant):
        k = k.astype(BF16)
        bias = bias_ref[variant, :, 0:n_keys]
        return [lax.dot_general(jnp.where(hm, q, 0.0).astype(BF16), k, nt_dims, preferred_element_type=F32) + bias
                for hm in heads]

    def weights(sc):
        m = jnp.max(sc, axis=1, keepdims=True)
        return jnp.exp2(sc - m).astype(BF16), m

    def values(pm, v):
        (pa, ma), (pb, mb) = pm
        oa = jnp.dot(pa, jnp.where(heads[0], v, 1.0).astype(BF16), preferred_element_type=F32)
        ob = jnp.dot(pb, jnp.where(heads[1], v, 1.0).astype(BF16), preferred_element_type=F32)
        num = jnp.where(head0, oa, ob)
        den = pltpu.roll(jnp.where(head0, ob, oa), HEAD_DIM, 1)
        return num, den, jnp.where(head0, ma, mb)

    def window(i, sub_len, n_keys):
        n_blocks = sub_len // QBLK
        q0 = (i % n_blocks) * QBLK
        return i // n_blocks, q0, jnp.clip(q0 - HALF_WINDOW, 0, sub_len - n_keys)

    def merged(part, other):
        (na, da, ma), (nb, db, mb) = part, other
        top = jnp.maximum(ma, mb)
        wa, wb = jnp.exp2(ma - top), jnp.exp2(mb - top)
        return wa * na + wb * nb, wa * da + wb * db, top

    def load(refs, idx):
        return tuple(r[idx + (slice(None),)] for r in refs)

    def store(refs, idx, part):
        for r, val in zip(refs, part):
            r[idx + (slice(None),)] = val

    def dilated16(i):
        c, r = i // DEINT, i % DEINT
        rows = pl.ds(r, QBLK, stride=DEINT)
        return (qd_ref[c, rows, :], kd_ref[c, rows, :], vd_ref[c, rows, :], QBLK, 0), (c, rows)

    def finish16(where, part):
        store(wide_scr, where, part)

    def dilated4(i):
        c, q0, k0 = window(i, sub4, 2 * QBLK)
        q_rows = pl.ds(pl.multiple_of(q0, QBLK), QBLK)
        k_rows = pl.ds(pl.multiple_of(k0, HALF_WINDOW), 2 * QBLK)
        return ((qd_ref[c, q_rows, :], kd_ref[c, k_rows, :], vd_ref[c, k_rows, :], 2 * QBLK,
                 (q0 - k0) // HALF_WINDOW), (c, q_rows, q0))

    def finish4(where, part):
        c, q_rows, q0 = where
        store(seq_scr, (pl.ds(c + DEINT * q0, QBLK, stride=DEINT),), merged(part, load(wide_scr, (c, q_rows))))

    def natural(i):
        _, q0, k0 = window(i, s_len, 2 * QBLK)
        q_rows = pl.ds(pl.multiple_of(q0, QBLK), QBLK)
        k_rows = pl.ds(pl.multiple_of(k0, HALF_WINDOW), 2 * QBLK)
        return ((q_ref[0, q_rows, :], k_ref[0, k_rows, :], v_ref[0, k_rows, :], 2 * QBLK,
                 (q0 - k0) // HALF_WINDOW), q_rows)

    def finish1(q_rows, part):
        num, den, _ = merged(part, load(seq_scr, (q_rows,)))
        o_ref[0, q_rows, :] = num / den

    for operands, finish, n_total in ((dilated16, finish16, DEINT * DEINT), (dilated4, finish4, s_len // QBLK),
                                      (natural, finish1, s_len // QBLK)):
        def group(g, carry, operands=operands, finish=finish):
            blocks, scs, outs = [], [], []
            for u in range(ATTN_UNROLL + ATTN_SKEW):
                if u < ATTN_UNROLL:
                    blocks.append(operands(g * ATTN_UNROLL + u))
                    q, k, _, n_keys, variant = blocks[u][0]
                    scs.append(scores(q, k, n_keys, variant))
                if u >= ATTN_SKEW:
                    d = u - ATTN_SKEW
                    outs.append(values([weights(sc) for sc in scs[d]], blocks[d][0][2]))
            for (_, where), part in zip(blocks, outs):
                finish(where, part)
            return carry

        lax.fori_loop(0, n_total // ATTN_UNROLL, group, 0)


def _attention(q, k, v, expert_weights):
    b, s, a_w = q.shape
    assert DILATIONS == (1, DEINT, DEINT * DEINT) and s // DILATIONS[-1] == QBLK
    rel = jnp.arange(QBLK)[:, None] - jnp.arange(2 * QBLK)[None, :]
    bias = jnp.stack([jnp.where(jnp.abs(rel + HALF_WINDOW * var) <= HALF_WINDOW, 0.0, NEG_INF)
                      for var in range(3)]).astype(F32)
    n_pairs = a_w // LANES
    n_steps = b * n_pairs
    spec = pl.BlockSpec((1, s, LANES), lambda i, j: (i, 0, j))
    sliced = [w.reshape(n_steps, -1, w.shape[2]) for w in expert_weights]
    w_specs = [pl.BlockSpec((1,) + w.shape[1:], lambda i, j: (i * n_pairs + j, 0, 0)) for w in sliced]
    outs = pl.pallas_call(
        _attn_kernel,
        grid=(b, n_pairs),
        in_specs=[spec] * 3 + [pl.BlockSpec(bias.shape, lambda i, j: (0, 0, 0))] + w_specs,
        out_specs=[spec] + w_specs,
        out_shape=[jax.ShapeDtypeStruct((b, s, a_w), F32)] + [jax.ShapeDtypeStruct(w.shape, BF16) for w in sliced],
        scratch_shapes=[pltpu.VMEM((DEINT, s // DEINT, LANES), F32)] * 6 + [pltpu.VMEM((s, LANES), F32)] * 3,
        compiler_params=_params("arbitrary", "arbitrary"),
        name="dilated_attn",
    )(q, k, v, bias, *sliced)
    return outs[0], [o.reshape(w.shape) for o, w in zip(outs[1:], expert_weights)]


def _outproj_kernel(x_ref, attn_ref, u_ref, dww_ref, dwb_ref, lng_ref, lnb_ref, wout_ref, g2_ref, wr_ref,
                    x1_ref, h2_ref, aff_ref, upad_ref, conv_ref):
    j = pl.program_id(1)
    s_len = u_ref.shape[1]
    tile = x_ref.shape[1]
    n_slabs = CONV_WIDTH // LANES
    shift = CONV_PAD - CONV_KERNEL // 2

    def conv_tile(t):
        base = t * tile

        def accumulate(i, carry):
            r0 = (i // n_slabs) * CONV_ROWS
            c = i % n_slabs
            acc = jnp.zeros((CONV_ROWS, LANES), F32) + dwb_ref[c]
            for tap in range(CONV_KERNEL):
                acc = acc + dww_ref[c, tap:tap + 1, :] * upad_ref[c, pl.ds(base + r0 + shift + tap, CONV_ROWS), :]
            conv_ref[c, pl.ds(pl.multiple_of(r0, CONV_ROWS), CONV_ROWS), :] = acc
            return carry

        lax.fori_loop(0, (tile // CONV_ROWS) * n_slabs, accumulate, 0)

    @pl.when(j == 0)
    def _start_sequence():
        for c in range(n_slabs):
            upad_ref[c, 0:CONV_PAD, :] = jnp.zeros((CONV_PAD, LANES), F32)
            upad_ref[c, CONV_PAD + s_len:2 * CONV_PAD + s_len, :] = jnp.zeros((CONV_PAD, LANES), F32)
            upad_ref[c, CONV_PAD:CONV_PAD + s_len, :] = u_ref[0, :, c * LANES:(c + 1) * LANES]

    conv_tile(j)
    y = jnp.concatenate([conv_ref[c] for c in range(n_slabs)], axis=1)
    mu = jnp.mean(y, axis=-1, keepdims=True)
    yc = y - mu
    y = yc * lax.rsqrt(jnp.mean(yc * yc, axis=-1, keepdims=True) + EPS)
    y = y * lng_ref[...] + lnb_ref[...]
    mixed = jnp.concatenate([attn_ref[0], y / (1.0 + jnp.exp(-y))], axis=1).astype(BF16)
    x1 = x_ref[0] + jnp.dot(mixed, wout_ref[...], preferred_element_type=F32)
    x1_ref[0] = x1
    ms = jnp.mean(x1 * x1, axis=-1, keepdims=True)
    h2 = (x1 * lax.rsqrt(ms + EPS) * g2_ref[...]).astype(BF16)
    h2_ref[0] = h2
    logits = lax.dot_general(wr_ref[...], h2, (((1,), (1,)), ((), ())), preferred_element_type=F32)
    mx = jnp.max(logits, axis=0, keepdims=True)
    ex = jnp.exp(logits - mx)
    aff_ref[0] = ex / jnp.sum(ex, axis=0, keepdims=True)


def _outproj(x, attn, u, conv_dw_w, conv_dw_b, conv_ln_g, conv_ln_b, w_out, norm2_g, w_router, tile):
    b, s, d = x.shape
    const = lambda shape: pl.BlockSpec(shape, lambda i, j: (0,) * len(shape))
    row = lambda width: pl.BlockSpec((1, tile, width), lambda i, j: (i, j, 0))
    vec = lambda a: a.astype(F32)[None, :]
    n_slabs = CONV_WIDTH // LANES
    slabs = lambda a: a.astype(F32).reshape(-1, n_slabs, LANES).swapaxes(0, 1)
    return pl.pallas_call(
        _outproj_kernel,
        grid=(b, s // tile),
        in_specs=[row(d), row(ATTN_WIDTH),
                  pl.BlockSpec((1, s, CONV_WIDTH), lambda i, j: (i, 0, 0)),
                  const((n_slabs, CONV_KERNEL, LANES)), const((n_slabs, 1, LANES)), const((1, CONV_WIDTH)),
                  const((1, CONV_WIDTH)), const((ATTN_WIDTH + CONV_WIDTH, d)), const((1, d)),
                  const((N_EXPERTS, d))],
        out_specs=[row(d), row(d), pl.BlockSpec((1, N_EXPERTS, tile), lambda i, j: (i, 0, j))],
        out_shape=[jax.ShapeDtypeStruct((b, s, d), F32), jax.ShapeDtypeStruct((b, s, d), BF16),
                   jax.ShapeDtypeStruct((b, N_EXPERTS, s), F32)],
        scratch_shapes=[pltpu.VMEM((n_slabs, s + 2 * CONV_PAD, LANES), F32),
                        pltpu.VMEM((n_slabs, tile, LANES), F32)],
        compiler_params=_params("arbitrary", "arbitrary"),
        name="outproj",
    )(x, attn, u, slabs(conv_dw_w), slabs(conv_dw_b[None, :]), vec(conv_ln_g), vec(conv_ln_b),
      w_out.astype(BF16), vec(norm2_g), w_router.T.astype(BF16))


def _route_kernel(aff_ref, slot_ref, cum_ref, tri_ref, *, capacity, tile):
    rows, s_len = aff_ref.shape

    @pl.when(pl.program_id(0) == 0)
    def _build_prefix_matrix():
        tri_ref[...] = jnp.where(lax.broadcasted_iota(jnp.int32, (s_len, s_len), 0)
                                 <= lax.broadcasted_iota(jnp.int32, (s_len, s_len), 1), 1.0, 0.0).astype(BF16)

    aff = aff_ref[...]

    def count(mask):
        return jnp.sum(jnp.where(mask, 1.0, 0.0), axis=1, keepdims=True)

    def search(i, thr_bits):
        cand_bits = thr_bits | jnp.left_shift(jnp.int32(1), 30 - i)
        return jnp.where(count(aff >= pltpu.bitcast(cand_bits, F32)) >= capacity, cand_bits, thr_bits)

    thr = pltpu.bitcast(lax.fori_loop(0, 31, search, jnp.zeros((rows, 1), jnp.int32)), F32)
    above = aff > thr
    tied = aff == thr
    need = capacity - count(above)

    def prefix(mask):
        return jnp.dot(jnp.where(mask, 1.0, 0.0).astype(BF16), tri_ref[...], preferred_element_type=F32)

    sel = above | (tied & (prefix(tied) <= need))
    sel_bf16 = jnp.where(sel, 1.0, 0.0).astype(BF16)
    pos = jnp.dot(sel_bf16, tri_ref[...], preferred_element_type=F32) - 1.0
    slot_ref[...] = jnp.where(sel, pos, -1.0)
    before = jnp.where(lax.broadcasted_iota(jnp.int32, (s_len, LANES), 0)
                       < lax.broadcasted_iota(jnp.int32, (s_len, LANES), 1) * tile, 1.0, 0.0).astype(BF16)
    cum_ref[...] = jnp.dot(sel_bf16, before, preferred_element_type=F32).astype(jnp.int32)


def _route(aff, capacity, rows_per_step, tile):
    r, s = aff.shape
    return pl.pallas_call(
        functools.partial(_route_kernel, capacity=capacity, tile=tile),
        grid=(r // rows_per_step,),
        in_specs=[pl.BlockSpec((rows_per_step, s), lambda i: (i, 0))],
        out_specs=[pl.BlockSpec((rows_per_step, s), lambda i: (i, 0)),
                   pl.BlockSpec((rows_per_step, LANES), lambda i: (i, 0))],
        out_shape=[jax.ShapeDtypeStruct((r, s), F32), jax.ShapeDtypeStruct((r, LANES), jnp.int32)],
        scratch_shapes=[pltpu.VMEM((s, s), BF16)],
        compiler_params=_params("arbitrary"),
        name="route",
    )(aff)


def _slot_window(cum_ref, row, t, capacity):
    win = capacity // WINDOW_DIV
    first = cum_ref[row * CUM_STRIDE + t]
    start = jnp.minimum(lax.shift_left(lax.shift_right_logical(first, SLOT_ALIGN_LOG2), SLOT_ALIGN_LOG2),
                        capacity - win)
    return start, cum_ref[row * CUM_STRIDE + t + 1] <= start + win


def _ffn_kernel(cum_ref, h_ref, slot_ref, aff_ref, w1_ref, w3_ref, w2_ref, ye_ref, xe_ref, *, capacity, tile):
    i, j = pl.program_id(0), pl.program_id(1)
    s_len = h_ref.shape[1]
    win = capacity // WINDOW_DIV
    windows = [_slot_window(cum_ref, j * pl.num_programs(0) + i, t, capacity) for t in range(s_len // tile)]
    all_fit = functools.reduce(jnp.logical_and, [fit for _, fit in windows])
    onehot = slot_ref[0, 0] == lax.broadcasted_iota(jnp.int32, (capacity, s_len), 0).astype(F32)
    gate = jnp.sum(jnp.where(onehot, aff_ref[0, 0], 0.0), axis=1, keepdims=True)

    def expert(xe):
        a = jnp.dot(xe, w1_ref[0], preferred_element_type=F32)
        g = jnp.dot(xe, w3_ref[0], preferred_element_type=F32)
        hid = (a / (1.0 + jnp.exp(-a)) * g).astype(BF16)
        out = jnp.dot(hid, w2_ref[0], preferred_element_type=F32)
        return (out * gate).astype(BF16)

    xe_ref[...] = jnp.zeros(xe_ref.shape, F32)
    c_iota = lax.broadcasted_iota(jnp.int32, (win, tile), 0).astype(F32)
    for t, (start, _) in enumerate(windows):
        start = pl.multiple_of(start, 1 << SLOT_ALIGN_LOG2)
        cols = slice(t * tile, (t + 1) * tile)
        hot = jnp.where(slot_ref[0, 0, :, cols] - start.astype(F32) == c_iota, 1.0, 0.0).astype(BF16)
        xe_ref[pl.ds(start, win), :] += jnp.dot(hot, h_ref[0, cols, :], preferred_element_type=F32)
    ye_ref[0, 0] = expert(xe_ref[...].astype(BF16))

    @pl.when(jnp.logical_not(all_fit))
    def _redo_with_full_gather():
        xe = jnp.dot(jnp.where(onehot, 1.0, 0.0).astype(BF16), h_ref[0], preferred_element_type=F32)
        ye_ref[0, 0] = expert(xe.astype(BF16))


def _ffn(cum_table, h2, slot4, aff4, w1, w3, w2, capacity, tile):
    b, s, d = h2.shape
    e, _, f = w1.shape
    resident = lambda shape: pl.BlockSpec(shape, lambda i, j, cum: (i, 0, 0))
    sel_spec = pl.BlockSpec((1, 1, 1, s), lambda i, j, cum: (j, i, 0, 0))
    return pl.pallas_call(
        functools.partial(_ffn_kernel, capacity=capacity, tile=tile),
        grid_spec=pltpu.PrefetchScalarGridSpec(
            num_scalar_prefetch=1,
            grid=(e, b),
            in_specs=[pl.BlockSpec((1, s, d), lambda i, j, cum: (j, 0, 0)), sel_spec, sel_spec,
                      resident((1, d, f)), resident((1, d, f)), resident((1, f, d))],
            out_specs=pl.BlockSpec((1, 1, capacity, d), lambda i, j, cum: (i, j, 0, 0)),
            scratch_shapes=[pltpu.VMEM((capacity, d), F32)]),
        out_shape=jax.ShapeDtypeStruct((e, b, capacity, d), BF16),
        compiler_params=_params("arbitrary", "arbitrary"),
        name="expert_ffn",
    )(cum_table, h2, slot4, aff4, w1, w3, w2)


def _combine_kernel(cum_ref, x1_ref, slot_ref, ye_ref, o_ref, *, capacity, sub_tile):
    i, j = pl.program_id(0), pl.program_id(1)
    tile = x1_ref.shape[1]
    win = capacity // WINDOW_DIV
    n_sub = tile // sub_tile
    windows = [[_slot_window(cum_ref, i * N_EXPERTS + ex, j * n_sub + u, capacity) for ex in range(N_EXPERTS)]
               for u in range(n_sub)]
    all_fit = functools.reduce(jnp.logical_and, [fit for sub in windows for _, fit in sub])
    contract_rows = (((0,), (0,)), ((), ()))

    @pl.when(all_fit)
    def _stacked_windows():
        c_iota = lax.broadcasted_iota(jnp.int32, (win, sub_tile), 0).astype(F32)
        for u in range(n_sub):
            tok = slice(u * sub_tile, (u + 1) * sub_tile)
            acc = x1_ref[0, tok, :]
            for e0 in range(0, N_EXPERTS, WINDOW_DIV):
                hots, rows = [], []
                for ex in range(e0, e0 + WINDOW_DIV):
                    start = pl.multiple_of(windows[u][ex][0], 1 << SLOT_ALIGN_LOG2)
                    hots.append(jnp.where(slot_ref[0, ex, :, tok] - start.astype(F32) == c_iota, 1.0, 0.0)
                                .astype(BF16))
                    rows.append(ye_ref[ex, 0, pl.ds(start, win), :])
                acc = acc + lax.dot_general(jnp.concatenate(hots, axis=0), jnp.concatenate(rows, axis=0),
                                            contract_rows, preferred_element_type=F32)
            o_ref[0, tok, :] = acc

    @pl.when(jnp.logical_not(all_fit))
    def _full():
        acc = x1_ref[0]
        c_iota = lax.broadcasted_iota(jnp.int32, (capacity, tile), 0).astype(F32)
        for ex in range(N_EXPERTS):
            onehot = jnp.where(slot_ref[0, ex] == c_iota, 1.0, 0.0).astype(BF16)
            acc = acc + lax.dot_general(onehot, ye_ref[ex, 0], contract_rows, preferred_element_type=F32)
        o_ref[0] = acc


def _combine(cum_table, x1, slot4, ye, capacity, tile, sub_tile):
    b, s, d = x1.shape
    return pl.pallas_call(
        functools.partial(_combine_kernel, capacity=capacity, sub_tile=sub_tile),
        grid_spec=pltpu.PrefetchScalarGridSpec(
            num_scalar_prefetch=1,
            grid=(b, s // tile),
            in_specs=[pl.BlockSpec((1, tile, d), lambda i, j, cum: (i, j, 0)),
                      pl.BlockSpec((1, N_EXPERTS, 1, tile), lambda i, j, cum: (i, 0, 0, j)),
                      pl.BlockSpec((N_EXPERTS, 1, capacity, d), lambda i, j, cum: (0, i, 0, 0))],
            out_specs=pl.BlockSpec((1, tile, d), lambda i, j, cum: (i, j, 0))),
        out_shape=jax.ShapeDtypeStruct((b, s, d), F32),
        compiler_params=_params("arbitrary", "arbitrary"),
        name="combine",
    )(cum_table, x1, slot4, ye)


def kernel(x, norm1_g, w_in, q_norm_g, k_norm_g, conv_dw_w, conv_dw_b, conv_ln_g, conv_ln_b,
           w_out, norm2_g, w_router, w1, w3, w2):
    b, s, d = x.shape
    capacity = CAPACITY_FACTOR * s // N_EXPERTS
    q, k, v, u = _inproj(x, norm1_g, w_in, q_norm_g, k_norm_g, tile=512)
    attn, (w1b, w3b, w2b) = _attention(q, k, v, (w1, w3, w2))
    x1, h2, aff = _outproj(x, attn, u, conv_dw_w, conv_dw_b, conv_ln_g, conv_ln_b,
                           w_out, norm2_g, w_router, tile=512)
    slot, cum = _route(aff.reshape(b * N_EXPERTS, s), capacity, rows_per_step=min(128, b * N_EXPERTS),
                       tile=MOE_TILE)
    assert s // MOE_TILE < CUM_STRIDE
    cum_table = cum[:, :CUM_STRIDE].reshape(-1)
    slot4 = slot.reshape(b, N_EXPERTS, 1, s)
    aff4 = aff.reshape(b, N_EXPERTS, 1, s)
    ye = _ffn(cum_table, h2, slot4, aff4, w1b, w3b, w2b, capacity, tile=MOE_TILE)
    return _combine(cum_table, x1, slot4, ye, capacity, tile=COMBINE_TILE, sub_tile=MOE_TILE)
```

```python
import functools

import jax
import jax.numpy as jnp
from jax import lax
from jax.experimental import pallas as pl
from jax.experimental.pallas import tpu as pltpu

F32 = jnp.float32
BF16 = jnp.bfloat16

HEAD_DIM = 64
ATTN_WIDTH = 512
CONV_WIDTH = 512
CONV_KERNEL = 31
CONV_PAD = 16
N_EXPERTS = 16
CAPACITY_FACTOR = 2
ROPE_THETA = 10000.0
EPS = 1e-6
NEG_INF = -1e30
HALF_WINDOW = 64
DILATIONS = (1, 4, 16)
QBLK = 128
ATTN_UNROLL = 16
ATTN_SKEW = 4
CONV_ROWS = 128
DEINT = 4
LANES = 128
MOE_TILE = 256
WINDOW_DIV = 4
COMBINE_TILE = 512
CUM_STRIDE = 16
SLOT_ALIGN_LOG2 = 4
LOG2E = 1.4426950408889634
VMEM_LIMIT = 56 * 1024 * 1024


def _params(*sem):
    return pltpu.CompilerParams(dimension_semantics=sem, vmem_limit_bytes=VMEM_LIMIT)


def _inproj_kernel(x_ref, g1_ref, win_ref, gq_ref, gk_ref, cos_ref, sin_ref, hsum_ref,
                   q_ref, k_ref, v_ref, u_ref):
    x = x_ref[0]
    ms = jnp.mean(x * x, axis=-1, keepdims=True)
    h = (x * lax.rsqrt(ms + EPS) * g1_ref[...]).astype(BF16)
    a_w = ATTN_WIDTH
    t = x.shape[0]
    cos = jnp.concatenate([cos_ref[...]] * (a_w // LANES), axis=1)
    sin = jnp.concatenate([sin_ref[...]] * (a_w // LANES), axis=1)
    lane = lax.broadcasted_iota(jnp.int32, (t, a_w), 1)
    first_half = (lane & (HEAD_DIM - 1)) < (HEAD_DIM // 2)
    hsum = hsum_ref[...]

    def head_sumsq(tz):
        sq = tz * tz
        return jnp.concatenate([jnp.dot(sq[:, c0:c0 + 256].astype(BF16), hsum, preferred_element_type=F32)
                                for c0 in range(0, a_w, 256)], axis=1)

    def head_norm_rope(tz, ssum, g):
        y = tz * lax.rsqrt(ssum * (1.0 / HEAD_DIM) + EPS) * g
        partner = jnp.where(first_half, pltpu.roll(y, a_w - HEAD_DIM // 2, 1), pltpu.roll(y, HEAD_DIM // 2, 1))
        return y * cos + partner * sin

    zq = jnp.dot(h, win_ref[:, 0:a_w], preferred_element_type=F32)
    zk = jnp.dot(h, win_ref[:, a_w:2 * a_w], preferred_element_type=F32)
    sq_sum, sk_sum = head_sumsq(zq), head_sumsq(zk)
    zr = jnp.dot(h, win_ref[:, 2 * a_w:], preferred_element_type=F32)
    q_ref[0] = head_norm_rope(zq, sq_sum, gq_ref[...]) * (HEAD_DIM ** -0.5 * LOG2E)
    k_ref[0] = head_norm_rope(zk, sk_sum, gk_ref[...])
    v_ref[0] = zr[:, 0:a_w]
    ga = zr[:, a_w:a_w + CONV_WIDTH]
    gb = zr[:, a_w + CONV_WIDTH:a_w + 2 * CONV_WIDTH]
    u_ref[0] = ga / (1.0 + jnp.exp(-gb))


def _inproj(x, norm1_g, w_in, q_norm_g, k_norm_g, tile):
    b, s, d = x.shape
    cols = w_in.shape[1]
    half = HEAD_DIM // 2
    inv_freq = ROPE_THETA ** (-jnp.arange(half, dtype=F32) / half)
    ang = jnp.arange(s, dtype=F32)[:, None] * inv_freq[None, :]
    cos = jnp.tile(jnp.cos(ang), (1, LANES // half))
    sin = jnp.tile(jnp.concatenate([-jnp.sin(ang), jnp.sin(ang)], axis=1), (1, LANES // HEAD_DIM))
    gq = jnp.tile(q_norm_g.astype(F32), ATTN_WIDTH // HEAD_DIM)[None, :]
    gk = jnp.tile(k_norm_g.astype(F32), ATTN_WIDTH // HEAD_DIM)[None, :]
    hid = jnp.arange(256) // HEAD_DIM
    hsum = (hid[:, None] == hid[None, :]).astype(BF16)
    out = jax.ShapeDtypeStruct((b, s, ATTN_WIDTH), F32)
    row_spec = pl.BlockSpec((1, tile, ATTN_WIDTH), lambda i, j: (i, j, 0))
    const = lambda shape: pl.BlockSpec(shape, lambda i, j: (0,) * len(shape))
    return pl.pallas_call(
        _inproj_kernel,
        grid=(b, s // tile),
        in_specs=[pl.BlockSpec((1, tile, d), lambda i, j: (i, j, 0)),
                  const((1, d)), const((d, cols)), const((1, ATTN_WIDTH)), const((1, ATTN_WIDTH)),
                  pl.BlockSpec((tile, LANES), lambda i, j: (j, 0)),
                  pl.BlockSpec((tile, LANES), lambda i, j: (j, 0)),
                  const((256, 256))],
        out_specs=[row_spec] * 4,
        out_shape=[out] * 4,
        compiler_params=_params("arbitrary", "arbitrary"),
        name="inproj",
    )(x, norm1_g.astype(F32)[None, :], w_in.astype(BF16), gq, gk, cos, sin, hsum)


def _attn_kernel(q_ref, k_ref, v_ref, bias_ref, w1_ref, w3_ref, w2_ref,
                 o_ref, w1b_ref, w3b_ref, w2b_ref, *scratch):
    w1b_ref[...] = w1_ref[...].astype(BF16)
    w3b_ref[...] = w3_ref[...].astype(BF16)
    w2b_ref[...] = w2_ref[...].astype(BF16)
    s_len = q_ref.shape[1]
    sub4 = s_len // DEINT
    qd_ref, kd_ref, vd_ref = scratch[0:3]
    wide_scr = scratch[3:6]
    seq_scr = scratch[6:9]
    lane = lax.broadcasted_iota(jnp.int32, (1, LANES), 1)
    head0 = lane < HEAD_DIM
    nt_dims = (((1,), (1,)), ((), ()))

    for c in range(DEINT):
        rows = pl.ds(c, sub4, stride=DEINT)
        qd_ref[c] = q_ref[0, rows, :]
        kd_ref[c] = k_ref[0, rows, :]
        vd_ref[c] = v_ref[0, rows, :]

    heads = (head0, jnp.logical_not(head0))

    def scores(q, k, n_keys, variant):
        q, k = q.astype(BF16), k.astype(BF16)
        bias = bias_ref[variant, :, 0:n_keys]
        return [lax.dot_general(jnp.where(hm, q, jnp.zeros((), BF16)), k, nt_dims, preferred_element_type=F32) + bias
                for hm in heads]

    def weights(sc):
        m = jnp.max(sc, axis=1, keepdims=True)
        return jnp.exp2(sc - m).astype(BF16), m

    def values(pm, v):
        (pa, ma), (pb, mb) = pm
        v = v.astype(BF16)
        one = jnp.ones((), BF16)
        oa = jnp.dot(pa, jnp.where(heads[0], v, one), preferred_element_type=F32)
        ob = jnp.dot(pb, jnp.where(heads[1], v, one), preferred_element_type=F32)
        num = jnp.where(head0, oa, ob)
        den = pltpu.roll(jnp.where(head0, ob, oa), HEAD_DIM, 1)
        return num, den, jnp.where(head0, ma, mb)

    def window(i, sub_len, n_keys):
        n_blocks = sub_len // QBLK
        q0 = (i % n_blocks) * QBLK
        return i // n_blocks, q0, jnp.clip(q0 - HALF_WINDOW, 0, sub_len - n_keys)

    def merged(part, other):
        (na, da, ma), (nb, db, mb) = part, other
        top = jnp.maximum(ma, mb)
        wa, wb = jnp.exp2(ma - top), jnp.exp2(mb - top)
        return wa * na + wb * nb, wa * da + wb * db, top

    def load(refs, idx):
        return tuple(r[idx + (slice(None),)] for r in refs)

    def store(refs, idx, part):
        for r, val in zip(refs, part):
            r[idx + (slice(None),)] = val

    def dilated16(i):
        c, r = i // DEINT, i % DEINT
        rows = pl.ds(r, QBLK, stride=DEINT)
        return (qd_ref[c, rows, :], kd_ref[c, rows, :], vd_ref[c, rows, :], QBLK, 0), (c, rows)

    def finish16(where, part):
        store(wide_scr, where, part)

    def dilated4(i):
        c, q0, k0 = window(i, sub4, 2 * QBLK)
        q_rows = pl.ds(pl.multiple_of(q0, QBLK), QBLK)
        k_rows = pl.ds(pl.multiple_of(k0, HALF_WINDOW), 2 * QBLK)
        return ((qd_ref[c, q_rows, :], kd_ref[c, k_rows, :], vd_ref[c, k_rows, :], 2 * QBLK,
                 (q0 - k0) // HALF_WINDOW), (c, q_rows, q0))

    def finish4(where, part):
        c, q_rows, q0 = where
        store(seq_scr, (pl.ds(c + DEINT * q0, QBLK, stride=DEINT),), merged(part, load(wide_scr, (c, q_rows))))

    def natural(i):
        _, q0, k0 = window(i, s_len, 2 * QBLK)
        q_rows = pl.ds(pl.multiple_of(q0, QBLK), QBLK)
        k_rows = pl.ds(pl.multiple_of(k0, HALF_WINDOW), 2 * QBLK)
        return ((q_ref[0, q_rows, :], k_ref[0, k_rows, :], v_ref[0, k_rows, :], 2 * QBLK,
                 (q0 - k0) // HALF_WINDOW), q_rows)

    def finish1(q_rows, part):
        num, den, _ = merged(part, load(seq_scr, (q_rows,)))
        o_ref[0, q_rows, :] = num / den

    for operands, finish, n_total in ((dilated16, finish16, DEINT * DEINT), (dilated4, finish4, s_len // QBLK),
                                      (natural, finish1, s_len // QBLK)):
        def group(g, carry, operands=operands, finish=finish):
            blocks, scs, outs = [], [], []
            for u in range(ATTN_UNROLL + ATTN_SKEW):
                if u < ATTN_UNROLL:
                    blocks.append(operands(g * ATTN_UNROLL + u))
                    q, k, _, n_keys, variant = blocks[u][0]
                    scs.append(scores(q, k, n_keys, variant))
                if u >= ATTN_SKEW:
                    d = u - ATTN_SKEW
                    outs.append(values([weights(sc) for sc in scs[d]], blocks[d][0][2]))
            for (_, where), part in zip(blocks, outs):
                finish(where, part)
            return carry

        lax.fori_loop(0, n_total // ATTN_UNROLL, group, 0)


def _attention(q, k, v, expert_weights):
    b, s, a_w = q.shape
    assert DILATIONS == (1, DEINT, DEINT * DEINT) and s // DILATIONS[-1] == QBLK
    rel = jnp.arange(QBLK)[:, None] - jnp.arange(2 * QBLK)[None, :]
    bias = jnp.stack([jnp.where(jnp.abs(rel + HALF_WINDOW * var) <= HALF_WINDOW, 0.0, NEG_INF)
                      for var in range(3)]).astype(F32)
    n_pairs = a_w // LANES
    n_steps = b * n_pairs
    spec = pl.BlockSpec((1, s, LANES), lambda i, j: (i, 0, j))
    sliced = [w.reshape(n_steps, -1, w.shape[2]) for w in expert_weights]
    w_specs = [pl.BlockSpec((1,) + w.shape[1:], lambda i, j: (i * n_pairs + j, 0, 0)) for w in sliced]
    outs = pl.pallas_call(
        _attn_kernel,
        grid=(b, n_pairs),
        in_specs=[spec] * 3 + [pl.BlockSpec(bias.shape, lambda i, j: (0, 0, 0))] + w_specs,
        out_specs=[spec] + w_specs,
        out_shape=[jax.ShapeDtypeStruct((b, s, a_w), F32)] + [jax.ShapeDtypeStruct(w.shape, BF16) for w in sliced],
        scratch_shapes=[pltpu.VMEM((DEINT, s // DEINT, LANES), F32)] * 6 + [pltpu.VMEM((s, LANES), F32)] * 3,
        compiler_params=_params("arbitrary", "arbitrary"),
        name="dilated_attn",
    )(q, k, v, bias, *sliced)
    return outs[0], [o.reshape(w.shape) for o, w in zip(outs[1:], expert_weights)]


def _outproj_kernel(x_ref, attn_ref, u_ref, dww_ref, dwb_ref, lng_ref, lnb_ref, wout_ref, g2_ref, wr_ref,
                    x1_ref, h2_ref, aff_ref, upad_ref, conv_ref):
    j = pl.program_id(1)
    s_len = u_ref.shape[1]
    tile = x_ref.shape[1]
    n_slabs = CONV_WIDTH // LANES
    shift = CONV_PAD - CONV_KERNEL // 2

    def conv_tile(t):
        base = t * tile

        def accumulate(i, carry):
            r0 = (i // n_slabs) * CONV_ROWS
            c = i % n_slabs
            acc = jnp.zeros((CONV_ROWS, LANES), F32) + dwb_ref[c]
            for tap in range(CONV_KERNEL):
                acc = acc + dww_ref[c, tap:tap + 1, :] * upad_ref[c, pl.ds(base + r0 + shift + tap, CONV_ROWS), :]
            conv_ref[c, pl.ds(pl.multiple_of(r0, CONV_ROWS), CONV_ROWS), :] = acc
            return carry

        lax.fori_loop(0, (tile // CONV_ROWS) * n_slabs, accumulate, 0)

    @pl.when(j == 0)
    def _start_sequence():
        for c in range(n_slabs):
            upad_ref[c, 0:CONV_PAD, :] = jnp.zeros((CONV_PAD, LANES), F32)
            upad_ref[c, CONV_PAD + s_len:2 * CONV_PAD + s_len, :] = jnp.zeros((CONV_PAD, LANES), F32)
            upad_ref[c, CONV_PAD:CONV_PAD + s_len, :] = u_ref[0, :, c * LANES:(c + 1) * LANES]

    conv_tile(j)
    y = jnp.concatenate([conv_ref[c] for c in range(n_slabs)], axis=1)
    mu = jnp.mean(y, axis=-1, keepdims=True)
    yc = y - mu
    y = yc * lax.rsqrt(jnp.mean(yc * yc, axis=-1, keepdims=True) + EPS)
    y = y * lng_ref[...] + lnb_ref[...]
    mixed = jnp.concatenate([attn_ref[0], y / (1.0 + jnp.exp(-y))], axis=1).astype(BF16)
    x1 = x_ref[0] + jnp.dot(mixed, wout_ref[...], preferred_element_type=F32)
    x1_ref[0] = x1
    ms = jnp.mean(x1 * x1, axis=-1, keepdims=True)
    h2 = (x1 * lax.rsqrt(ms + EPS) * g2_ref[...]).astype(BF16)
    h2_ref[0] = h2
    logits = lax.dot_general(wr_ref[...], h2, (((1,), (1,)), ((), ())), preferred_element_type=F32)
    mx = jnp.max(logits, axis=0, keepdims=True)
    ex = jnp.exp(logits - mx)
    aff_ref[0] = ex / jnp.sum(ex, axis=0, keepdims=True)


def _outproj(x, attn, u, conv_dw_w, conv_dw_b, conv_ln_g, conv_ln_b, w_out, norm2_g, w_router, tile):
    b, s, d = x.shape
    const = lambda shape: pl.BlockSpec(shape, lambda i, j: (0,) * len(shape))
    row = lambda width: pl.BlockSpec((1, tile, width), lambda i, j: (i, j, 0))
    vec = lambda a: a.astype(F32)[None, :]
    n_slabs = CONV_WIDTH // LANES
    slabs = lambda a: a.astype(F32).reshape(-1, n_slabs, LANES).swapaxes(0, 1)
    return pl.pallas_call(
        _outproj_kernel,
        grid=(b, s // tile),
        in_specs=[row(d), row(ATTN_WIDTH),
                  pl.BlockSpec((1, s, CONV_WIDTH), lambda i, j: (i, 0, 0)),
                  const((n_slabs, CONV_KERNEL, LANES)), const((n_slabs, 1, LANES)), const((1, CONV_WIDTH)),
                  const((1, CONV_WIDTH)), const((ATTN_WIDTH + CONV_WIDTH, d)), const((1, d)),
                  const((N_EXPERTS, d))],
        out_specs=[row(d), row(d), pl.BlockSpec((1, N_EXPERTS, tile), lambda i, j: (i, 0, j))],
        out_shape=[jax.ShapeDtypeStruct((b, s, d), F32), jax.ShapeDtypeStruct((b, s, d), BF16),
                   jax.ShapeDtypeStruct((b, N_EXPERTS, s), F32)],
        scratch_shapes=[pltpu.VMEM((n_slabs, s + 2 * CONV_PAD, LANES), F32),
                        pltpu.VMEM((n_slabs, tile, LANES), F32)],
        compiler_params=_params("arbitrary", "arbitrary"),
        name="outproj",
    )(x, attn, u, slabs(conv_dw_w), slabs(conv_dw_b[None, :]), vec(conv_ln_g), vec(conv_ln_b),
      w_out.astype(BF16), vec(norm2_g), w_router.T.astype(BF16))


def _route_kernel(aff_ref, slot_ref, cum_ref, tri_ref, *, capacity, tile):
    rows, s_len = aff_ref.shape

    @pl.when(pl.program_id(0) == 0)
    def _build_prefix_matrix():
        tri_ref[...] = jnp.where(lax.broadcasted_iota(jnp.int32, (s_len, s_len), 0)
                                 <= lax.broadcasted_iota(jnp.int32, (s_len, s_len), 1), 1.0, 0.0).astype(BF16)

    aff = aff_ref[...]

    def count(mask):
        return jnp.sum(jnp.where(mask, 1.0, 0.0), axis=1, keepdims=True)

    def search(i, thr_bits):
        cand_bits = thr_bits | jnp.left_shift(jnp.int32(1), 30 - i)
        return jnp.where(count(aff >= pltpu.bitcast(cand_bits, F32)) >= capacity, cand_bits, thr_bits)

    thr = pltpu.bitcast(lax.fori_loop(0, 31, search, jnp.zeros((rows, 1), jnp.int32)), F32)
    above = aff > thr
    tied = aff == thr
    need = capacity - count(above)

    def prefix(mask):
        return jnp.dot(jnp.where(mask, 1.0, 0.0).astype(BF16), tri_ref[...], preferred_element_type=F32)

    sel = above | (tied & (prefix(tied) <= need))
    sel_bf16 = jnp.where(sel, 1.0, 0.0).astype(BF16)
    pos = jnp.dot(sel_bf16, tri_ref[...], preferred_element_type=F32) - 1.0
    slot_ref[...] = jnp.where(sel, pos, -1.0)
    before = jnp.where(lax.broadcasted_iota(jnp.int32, (s_len, LANES), 0)
                       < lax.broadcasted_iota(jnp.int32, (s_len, LANES), 1) * tile, 1.0, 0.0).astype(BF16)
    cum_ref[...] = jnp.dot(sel_bf16, before, preferred_element_type=F32).astype(jnp.int32)


def _route(aff, capacity, rows_per_step, tile):
    r, s = aff.shape
    return pl.pallas_call(
        functools.partial(_route_kernel, capacity=capacity, tile=tile),
        grid=(r // rows_per_step,),
        in_specs=[pl.BlockSpec((rows_per_step, s), lambda i: (i, 0))],
        out_specs=[pl.BlockSpec((rows_per_step, s), lambda i: (i, 0)),
                   pl.BlockSpec((rows_per_step, LANES), lambda i: (i, 0))],
        out_shape=[jax.ShapeDtypeStruct((r, s), F32), jax.ShapeDtypeStruct((r, LANES), jnp.int32)],
        scratch_shapes=[pltpu.VMEM((s, s), BF16)],
        compiler_params=_params("arbitrary"),
        name="route",
    )(aff)


def _slot_window(cum_ref, row, t, capacity):
    win = capacity // WINDOW_DIV
    first = cum_ref[row * CUM_STRIDE + t]
    start = jnp.minimum(lax.shift_left(lax.shift_right_logical(first, SLOT_ALIGN_LOG2), SLOT_ALIGN_LOG2),
                        capacity - win)
    return start, cum_ref[row * CUM_STRIDE + t + 1] <= start + win


def _ffn_kernel(cum_ref, h_ref, slot_ref, aff_ref, w1_ref, w3_ref, w2_ref, ye_ref, xe_ref, *, capacity, tile):
    i, j = pl.program_id(0), pl.program_id(1)
    s_len = h_ref.shape[1]
    win = capacity // WINDOW_DIV
    windows = [_slot_window(cum_ref, j * pl.num_programs(0) + i, t, capacity) for t in range(s_len // tile)]
    all_fit = functools.reduce(jnp.logical_and, [fit for _, fit in windows])
    onehot = slot_ref[0, 0] == lax.broadcasted_iota(jnp.int32, (capacity, s_len), 0).astype(F32)
    gate = jnp.sum(jnp.where(onehot, aff_ref[0, 0], 0.0), axis=1, keepdims=True)

    def expert(xe):
        a = jnp.dot(xe, w1_ref[0], preferred_element_type=F32)
        g = jnp.dot(xe, w3_ref[0], preferred_element_type=F32)
        hid = (a / (1.0 + jnp.exp(-a)) * g).astype(BF16)
        out = jnp.dot(hid, w2_ref[0], preferred_element_type=F32)
        return (out * gate).astype(BF16)

    xe_ref[...] = jnp.zeros(xe_ref.shape, F32)
    c_iota = lax.broadcasted_iota(jnp.int32, (win, tile), 0).astype(F32)
    for t, (start, _) in enumerate(windows):
        start = pl.multiple_of(start, 1 << SLOT_ALIGN_LOG2)
        cols = slice(t * tile, (t + 1) * tile)
        hot = jnp.where(slot_ref[0, 0, :, cols] - start.astype(F32) == c_iota, 1.0, 0.0).astype(BF16)
        xe_ref[pl.ds(start, win), :] += jnp.dot(hot, h_ref[0, cols, :], preferred_element_type=F32)
    ye_ref[0, 0] = expert(xe_ref[...].astype(BF16))

    @pl.when(jnp.logical_not(all_fit))
    def _redo_with_full_gather():
        xe = jnp.dot(jnp.where(onehot, 1.0, 0.0).astype(BF16), h_ref[0], preferred_element_type=F32)
        ye_ref[0, 0] = expert(xe.astype(BF16))


def _ffn(cum_table, h2, slot4, aff4, w1, w3, w2, capacity, tile):
    b, s, d = h2.shape
    e, _, f = w1.shape
    resident = lambda shape: pl.BlockSpec(shape, lambda i, j, cum: (i, 0, 0))
    sel_spec = pl.BlockSpec((1, 1, 1, s), lambda i, j, cum: (j, i, 0, 0))
    return pl.pallas_call(
        functools.partial(_ffn_kernel, capacity=capacity, tile=tile),
        grid_spec=pltpu.PrefetchScalarGridSpec(
            num_scalar_prefetch=1,
            grid=(e, b),
            in_specs=[pl.BlockSpec((1, s, d), lambda i, j, cum: (j, 0, 0)), sel_spec, sel_spec,
                      resident((1, d, f)), resident((1, d, f)), resident((1, f, d))],
            out_specs=pl.BlockSpec((1, 1, capacity, d), lambda i, j, cum: (i, j, 0, 0)),
            scratch_shapes=[pltpu.VMEM((capacity, d), F32)]),
        out_shape=jax.ShapeDtypeStruct((e, b, capacity, d), BF16),
        compiler_params=_params("arbitrary", "arbitrary"),
        name="expert_ffn",
    )(cum_table, h2, slot4, aff4, w1, w3, w2)


def _combine_kernel(cum_ref, x1_ref, slot_ref, ye_ref, o_ref, *, capacity, sub_tile):
    i, j = pl.program_id(0), pl.program_id(1)
    tile = x1_ref.shape[1]
    win = capacity // WINDOW_DIV
    n_sub = tile // sub_tile
    windows = [[_slot_window(cum_ref, i * N_EXPERTS + ex, j * n_sub + u, capacity) for ex in range(N_EXPERTS)]
               for u in range(n_sub)]
    all_fit = functools.reduce(jnp.logical_and, [fit for sub in windows for _, fit in sub])
    contract_rows = (((0,), (0,)), ((), ()))

    @pl.when(all_fit)
    def _stacked_windows():
        c_iota = lax.broadcasted_iota(jnp.int32, (win, sub_tile), 0).astype(F32)
        for u in range(n_sub):
            tok = slice(u * sub_tile, (u + 1) * sub_tile)
            acc = x1_ref[0, tok, :]
            for e0 in range(0, N_EXPERTS, WINDOW_DIV):
                hots, rows = [], []
                for ex in range(e0, e0 + WINDOW_DIV):
                    start = pl.multiple_of(windows[u][ex][0], 1 << SLOT_ALIGN_LOG2)
                    hots.append(jnp.where(slot_ref[0, ex, :, tok] - start.astype(F32) == c_iota, 1.0, 0.0)
                                .astype(BF16))
                    rows.append(ye_ref[ex, 0, pl.ds(start, win), :])
                acc = acc + lax.dot_general(jnp.concatenate(hots, axis=0), jnp.concatenate(rows, axis=0),
                                            contract_rows, preferred_element_type=F32)
            o_ref[0, tok, :] = acc

    @pl.when(jnp.logical_not(all_fit))
    def _full():
        acc = x1_ref[0]
        c_iota = lax.broadcasted_iota(jnp.int32, (capacity, tile), 0).astype(F32)
        for ex in range(N_EXPERTS):
            onehot = jnp.where(slot_ref[0, ex] == c_iota, 1.0, 0.0).astype(BF16)
            acc = acc + lax.dot_general(onehot, ye_ref[ex, 0], contract_rows, preferred_element_type=F32)
        o_ref[0] = acc


def _combine(cum_table, x1, slot4, ye, capacity, tile, sub_tile):
    b, s, d = x1.shape
    return pl.pallas_call(
        functools.partial(_combine_kernel, capacity=capacity, sub_tile=sub_tile),
        grid_spec=pltpu.PrefetchScalarGridSpec(
            num_scalar_prefetch=1,
            grid=(b, s // tile),
            in_specs=[pl.BlockSpec((1, tile, d), lambda i, j, cum: (i, j, 0)),
                      pl.BlockSpec((1, N_EXPERTS, 1, tile), lambda i, j, cum: (i, 0, 0, j)),
                      pl.BlockSpec((N_EXPERTS, 1, capacity, d), lambda i, j, cum: (0, i, 0, 0))],
            out_specs=pl.BlockSpec((1, tile, d), lambda i, j, cum: (i, j, 0))),
        out_shape=jax.ShapeDtypeStruct((b, s, d), F32),
        compiler_params=_params("arbitrary", "arbitrary"),
        name="combine",
    )(cum_table, x1, slot4, ye)


def kernel(x, norm1_g, w_in, q_norm_g, k_norm_g, conv_dw_w, conv_dw_b, conv_ln_g, conv_ln_b,
           w_out, norm2_g, w_router, w1, w3, w2):
    b, s, d = x.shape
    capacity = CAPACITY_FACTOR * s // N_EXPERTS
    q, k, v, u = _inproj(x, norm1_g, w_in, q_norm_g, k_norm_g, tile=512)
    attn, (w1b, w3b, w2b) = _attention(q, k, v, (w1, w3, w2))
    x1, h2, aff = _outproj(x, attn, u, conv_dw_w, conv_dw_b, conv_ln_g, conv_ln_b,
                           w_out, norm2_g, w_router, tile=512)
    slot, cum = _route(aff.reshape(b * N_EXPERTS, s), capacity, rows_per_step=min(128, b * N_EXPERTS),
                       tile=MOE_TILE)
    assert s // MOE_TILE < CUM_STRIDE
    cum_table = cum[:, :CUM_STRIDE].reshape(-1)
    slot4 = slot.reshape(b, N_EXPERTS, 1, s)
    aff4 = aff.reshape(b, N_EXPERTS, 1, s)
    ye = _ffn(cum_table, h2, slot4, aff4, w1b, w3b, w2b, capacity, tile=MOE_TILE)
    return _combine(cum_table, x1, slot4, ye, capacity, tile=COMBINE_TILE, sub_tile=MOE_TILE)
```
